```python
import jax, jax.numpy as jnp
from jax import lax
import numpy as np

D_MODEL = 1024
BATCH = 4
SEQ = 8192
DEPTH = 1

CHUNK = 64
SUB_CHUNK = 16
HG_DK = 128
HG_HEADS = D_MODEL // HG_DK
HG_DV = D_MODEL // HG_HEADS
HG_FD = HG_HEADS * HG_DK
HG_DI = HG_HEADS * HG_DV
LB_TAIL_LOGIT = 2.0
GM_BLOCK = 128
GM_WIDTH = D_MODEL
GM_GROUPS = 8
GM_CG = GM_WIDTH // GM_GROUPS
D_FF = -(-8 * D_MODEL // (3 * 256)) * 256
IN_WIDTH = 2 * HG_FD + 2 * HG_DI + 2 * GM_WIDTH + 2 * D_MODEL
IN_SPLITS = (HG_FD, 2 * HG_FD, 2 * HG_FD + HG_DI, 2 * HG_FD + 2 * HG_DI,
             2 * HG_FD + 2 * HG_DI + GM_WIDTH, 2 * HG_FD + 2 * HG_DI + 2 * GM_WIDTH,
             2 * HG_FD + 2 * HG_DI + 2 * GM_WIDTH + D_MODEL)
DEEPNORM_ALPHA = (2.0 * DEPTH) ** 0.25
DEEPNORM_BETA = (8.0 * DEPTH) ** -0.25
LN_EPS = 1e-5
RMS_EPS = 1e-6

kernel_name = "hybrid_hgrn2_gmlp_deepnorm_adaln_block"


def _layer_norm(x):
    x = x.astype(jnp.float32)
    xc = x - jnp.mean(x, axis=-1, keepdims=True)
    return xc * lax.rsqrt(jnp.mean(xc * xc, axis=-1, keepdims=True) + LN_EPS)


def _to_chunks(t, heads):
    b, s, _ = t.shape
    return t.reshape(b, s // CHUNK, CHUNK, heads, -1).transpose(0, 3, 1, 2, 4)


def hgrn2_mixer(zq, zf, zi, zg, lower_bound, norm_w):
    b_, s_, _ = zq.shape
    n_sub = CHUNK // SUB_CHUNK
    f = lower_bound + (1.0 - lower_bound) * jax.nn.sigmoid(zf.astype(jnp.float32))
    q = _to_chunks(jax.nn.silu(zq.astype(jnp.float32)), HG_HEADS)
    k = _to_chunks(1.0 - f, HG_HEADS)
    log_f = _to_chunks(jnp.log(f), HG_HEADS)
    v = _to_chunks(zi.astype(jnp.float32), HG_HEADS)
    cum = jnp.cumsum(log_f, axis=3)
    cum_last = cum[:, :, :, -1:, :]

    u_chunk = jnp.einsum('bhnlk,bhnlv->bhnkv', k * jnp.exp(cum_last - cum), v)
    decay = jnp.exp(cum_last[:, :, :, 0, :])

    def step(state, inp):
        dec, upd = inp
        return dec[..., None] * state + upd, state

    state0 = jnp.zeros((b_, HG_HEADS, HG_DK, HG_DV), jnp.float32)
    _, s_prev = lax.scan(step, state0, (jnp.moveaxis(decay, 2, 0), jnp.moveaxis(u_chunk, 2, 0)))
    s_prev = jnp.moveaxis(s_prev, 0, 2)
    o_inter = jnp.einsum('bhnlk,bhnkv->bhnlv', q * jnp.exp(cum), s_prev)

    shp = q.shape[:3] + (n_sub, SUB_CHUNK, HG_DK)
    ref = (cum - log_f)[:, :, :, ::SUB_CHUNK, :]
    q_sub = q.reshape(shp) * jnp.exp(cum.reshape(shp) - ref[:, :, :, :, None, :])
    key_pos = jnp.arange(CHUNK)
    sub_idx = jnp.arange(n_sub)
    key_ok = key_pos[None, :] < (sub_idx[:, None] + 1) * SUB_CHUNK
    expo = ref[:, :, :, :, None, :] - cum[:, :, :, None, :, :]
    k_sub = k[:, :, :, None] * jnp.exp(jnp.where(key_ok[:, :, None], expo, -jnp.inf))
    scores = jnp.einsum('bhnimk,bhnisk->bhnims', q_sub, k_sub)
    q_pos = sub_idx[:, None] * SUB_CHUNK + jnp.arange(SUB_CHUNK)[None, :]
    causal = key_pos[None, None, :] <= q_pos[:, :, None]
    scores = jnp.where(causal, scores, 0.0)
    o_intra = jnp.einsum('bhnims,bhnsv->bhnimv', scores, v).reshape(o_inter.shape)

    o = (o_inter + o_intra).transpose(0, 2, 3, 1, 4).reshape(b_, s_, HG_HEADS, HG_DV)
    o = o * lax.rsqrt(jnp.mean(o * o, axis=-1, keepdims=True) + RMS_EPS) * norm_w
    gate = jax.nn.silu(zg.astype(jnp.float32)).reshape(b_, s_, HG_HEADS, HG_DV)
    return (o * gate).reshape(b_, s_, HG_DI)


def gmlp_mixer(zu, zv, ln_w, ln_b, w_s, b_s):
    b_, s_, _ = zu.shape
    u = jax.nn.gelu(zu.astype(jnp.float32), approximate=False)
    v = jax.nn.gelu(zv.astype(jnp.float32), approximate=False)
    v = _layer_norm(v) * ln_w + ln_b
    v = v.reshape(b_, s_ // GM_BLOCK, GM_BLOCK, GM_GROUPS, GM_CG)
    pos = jnp.arange(GM_BLOCK) // CHUNK
    mask = pos[:, None] >= pos[None, :]
    w = jnp.where(mask[None], w_s, 0.0)
    sv = jnp.einsum('gts,bnsgc->bntgc', w, v) + b_s.T[None, None, :, :, None]
    return u * sv.reshape(b_, s_, GM_WIDTH)


def setup_inputs(seed: int = 0) -> dict:
    key = jax.random.key(seed)
    ks = jax.random.split(key, 24)

    def nrm(k, shape, scale):
        return jax.random.normal(k, shape, jnp.float32) * scale

    L = DEPTH
    return {
        "x": nrm(ks[0], (BATCH, SEQ, D_MODEL), 1.0),
        "c": nrm(ks[1], (BATCH, D_MODEL), 1.0),
        "w_ada": nrm(ks[2], (L, D_MODEL, 6 * D_MODEL), D_MODEL ** -0.5),
        "b_ada": nrm(ks[3], (L, 6 * D_MODEL), 0.01),
        "w_in": nrm(ks[4], (L, D_MODEL, IN_WIDTH), D_MODEL ** -0.5),
        "b_gate": nrm(ks[5], (L, 2, D_MODEL), 0.01),
        "hgrn_lb_logits": nrm(ks[6], (L + 1, HG_FD), 0.1).at[L].add(LB_TAIL_LOGIT),
        "hgrn_norm_w": 1.0 + nrm(ks[7], (L, HG_DV), 0.02),
        "w_proj_a": nrm(ks[8], (L, HG_DI, D_MODEL), HG_DI ** -0.5),
        "gmlp_ln_w": 1.0 + nrm(ks[9], (L, GM_WIDTH), 0.02),
        "gmlp_ln_b": nrm(ks[10], (L, GM_WIDTH), 0.02),
        "gmlp_ws": nrm(ks[11], (L, GM_GROUPS, GM_BLOCK, GM_BLOCK), GM_BLOCK ** -0.5),
        "gmlp_bs": 1.0 + nrm(ks[12], (L, GM_GROUPS, GM_BLOCK), 0.1),
        "w_proj_b": nrm(ks[13], (L, GM_WIDTH, D_MODEL), GM_WIDTH ** -0.5),
        "w_out": nrm(ks[14], (L, D_MODEL, D_MODEL), DEEPNORM_BETA * D_MODEL ** -0.5),
        "ln1_w": 1.0 + nrm(ks[15], (L, D_MODEL), 0.02),
        "ln1_b": nrm(ks[16], (L, D_MODEL), 0.02),
        "w_ffn_in": nrm(ks[17], (L, D_MODEL, 2 * D_FF), D_MODEL ** -0.5),
        "w_ffn_out": nrm(ks[18], (L, D_FF, D_MODEL), DEEPNORM_BETA * D_FF ** -0.5),
        "ln2_w": 1.0 + nrm(ks[19], (L, D_MODEL), 0.02),
        "ln2_b": nrm(ks[20], (L, D_MODEL), 0.02),
    }


def reference(x, c, w_ada, b_ada, w_in, b_gate, hgrn_lb_logits, hgrn_norm_w, w_proj_a,
              gmlp_ln_w, gmlp_ln_b, gmlp_ws, gmlp_bs, w_proj_b, w_out, ln1_w, ln1_b,
              w_ffn_in, w_ffn_out, ln2_w, ln2_b):
    lower_bounds = jnp.cumsum(jax.nn.softmax(hgrn_lb_logits.astype(jnp.float32), axis=0), axis=0)
    h = x.astype(jnp.float32)
    cond = jax.nn.silu(c.astype(jnp.float32))
    for l in range(DEPTH):
        mod = cond @ w_ada[l] + b_ada[l]
        sh1, sc1, g1, sh2, sc2, g2 = [m[:, None, :] for m in jnp.split(mod, 6, axis=-1)]

        u = _layer_norm(h) * (1.0 + sc1) + sh1
        z = u @ w_in[l]
        zq, zf, zi, zg, zu, zv, zga, zgb = jnp.split(z, IN_SPLITS, axis=-1)
        y_a = hgrn2_mixer(zq, zf, zi, zg, lower_bounds[l], hgrn_norm_w[l]) @ w_proj_a[l]
        y_b = gmlp_mixer(zu, zv, gmlp_ln_w[l], gmlp_ln_b[l], gmlp_ws[l], gmlp_bs[l]) @ w_proj_b[l]
        gate_a = jax.nn.sigmoid(zga + b_gate[l, 0])
        gate_b = jax.nn.sigmoid(zgb + b_gate[l, 1])
        mix = (gate_a * y_a + gate_b * y_b) @ w_out[l]
        h = _layer_norm(DEEPNORM_ALPHA * h + g1 * mix) * ln1_w[l] + ln1_b[l]

        u2 = _layer_norm(h) * (1.0 + sc2) + sh2
        a, bb = jnp.split(u2 @ w_ffn_in[l], 2, axis=-1)
        ffn = (jax.nn.silu(a) * bb) @ w_ffn_out[l]
        h = _layer_norm(DEEPNORM_ALPHA * h + g2 * ffn) * ln2_w[l] + ln2_b[l]
    return h.astype(x.dtype)
```

```python
import functools

import jax
import jax.numpy as jnp
from jax import lax
from jax.experimental import pallas as pl
from jax.experimental.pallas import tpu as pltpu

D_MODEL = 1024
CHUNK = 64
SUB_CHUNK = 16
N_SUB = CHUNK // SUB_CHUNK
HG_DK = 128
HG_HEADS = D_MODEL // HG_DK
GM_BLOCK = 128
GM_GROUPS = 8
GM_CG = D_MODEL // GM_GROUPS
D_FF = 2816
DEEPNORM_ALPHA = 2.0 ** 0.25
LN_EPS = 1e-5
RMS_EPS = 1e-6

COL_Q, COL_F, COL_I, COL_G, COL_U, COL_V, COL_GA, COL_GB = (i * D_MODEL for i in range(8))

MIX_TM = 256
FFN_TM = 512
HEAD_PAIR = 2 * HG_DK
VMEM_LIMIT = 60 * 1024 * 1024

_F32 = jnp.float32
_BF16 = jnp.bfloat16


def _dot(a, b):
    return jnp.dot(a, b, preferred_element_type=_F32)


def _dot_nt(a, b):
    return lax.dot_general(a, b, (((1,), (1,)), ((), ())), preferred_element_type=_F32)


def _dot_tn(a, b):
    return lax.dot_general(a, b, (((0,), (0,)), ((), ())), preferred_element_type=_F32)


def _layer_norm(x):
    xc = x - jnp.mean(x, axis=-1, keepdims=True)
    return xc * lax.rsqrt(jnp.mean(xc * xc, axis=-1, keepdims=True) + LN_EPS)


def _sigmoid(x):
    return 1.0 / (1.0 + jnp.exp(-x))


def _silu(x):
    return x * _sigmoid(x)


def _gelu(x):
    return 0.5 * x * (1.0 + lax.erf(x * (2.0 ** -0.5)))


def _prep_kernel(c_ref, w_ref, b_ref, lbl_ref, mod_ref, lb_ref):
    cond = _silu(c_ref[...])
    mod_ref[0] = jnp.dot(cond, w_ref[...], preferred_element_type=_F32,
                         precision=lax.Precision.HIGHEST) + b_ref[0]
    logits = lbl_ref[...]
    e = jnp.exp(logits - jnp.max(logits, axis=0, keepdims=True))
    lb_ref[...] = e[0:1] / jnp.sum(e, axis=0, keepdims=True)


def _prep(c, w_ada, b_ada, lb_logits):
    b = c.shape[0]
    return pl.pallas_call(
        _prep_kernel,
        grid=(6,),
        in_specs=[
            pl.BlockSpec((b, D_MODEL), lambda j: (0, 0)),
            pl.BlockSpec((D_MODEL, D_MODEL), lambda j: (0, j)),
            pl.BlockSpec((1, 1, D_MODEL), lambda j: (j, 0, 0)),
            pl.BlockSpec(lb_logits.shape, lambda j: (0, 0)),
        ],
        out_specs=[
            pl.BlockSpec((1, b, D_MODEL), lambda j: (j, 0, 0)),
            pl.BlockSpec((1, D_MODEL), lambda j: (0, 0)),
        ],
        out_shape=[
            jax.ShapeDtypeStruct((6, b, D_MODEL), _F32),
            jax.ShapeDtypeStruct((1, D_MODEL), _F32),
        ],
        name="prep",
    )(c, w_ada, b_ada.reshape(6, 1, D_MODEL), lb_logits)


def _hgrn2_chunk_head(q, k, cum, v, zg, st, norm_w):
    cl = cum[CHUNK - 1:CHUNK]
    v_b = v.astype(_BF16)
    qd = (q * jnp.exp(cum)).astype(_BF16)
    kd = (k * jnp.exp(cl - cum)).astype(_BF16)
    o_inter = _dot_nt(qd, st.astype(_BF16))
    st_new = st * jnp.exp(cl) + _dot_tn(v_b, kd)

    outs = []
    for i in range(N_SUB):
        lo, hi = i * SUB_CHUNK, (i + 1) * SUB_CHUNK
        cum_q = cum[lo:hi]
        cum_k = cum[0:hi]
        if i == 0:
            qs = q[lo:hi] * jnp.exp(cum_q)
            ks = k[0:hi] * jnp.exp(-cum_k)
        else:
            ref = cum[lo - 1:lo]
            qs = q[lo:hi] * jnp.exp(cum_q - ref)
            ks = k[0:hi] * jnp.exp(ref - cum_k)
        sc = _dot_nt(qs.astype(_BF16), ks.astype(_BF16))
        row = lax.broadcasted_iota(jnp.int32, sc.shape, 0) + lo
        col = lax.broadcasted_iota(jnp.int32, sc.shape, 1)
        sc = jnp.where(col <= row, sc, 0.0)
        outs.append(_dot(sc.astype(_BF16), v_b[0:hi]))
    o = o_inter + jnp.concatenate(outs, axis=0)
    o = o * lax.rsqrt(jnp.mean(o * o, axis=-1, keepdims=True) + RMS_EPS) * norm_w
    return o * _silu(zg), st_new


def _mixer_kernel(x_ref, mod_ref, lb_ref, w_in_ref, tri_ref, bgate_ref, normw_ref, wpa_ref,
                  lnw_ref, lnb_ref, ws_ref, bs_ref, wpb_ref, wout_ref, ln1w_ref, ln1b_ref,
                  out_ref, st_ref, u_ref, hg_ref, gm_ref):
    @pl.when(pl.program_id(1) == 0)
    def _():
        st_ref[...] = jnp.zeros_like(st_ref)

    h = x_ref[0]
    sh1, sc1, g1 = mod_ref[0, 0:1], mod_ref[0, 1:2], mod_ref[0, 2:3]
    u_ref[...] = (_layer_norm(h) * (1.0 + sc1) + sh1).astype(_BF16)

    tri = tri_ref[...]
    norm_w = normw_ref[...]

    def pair_body(p, carry):
        off = pl.multiple_of(p * HEAD_PAIR, HEAD_PAIR)
        u = u_ref[...]
        zq = _dot(u, w_in_ref[:, pl.ds(COL_Q + off, HEAD_PAIR)])
        zf = _dot(u, w_in_ref[:, pl.ds(COL_F + off, HEAD_PAIR)])
        zi = _dot(u, w_in_ref[:, pl.ds(COL_I + off, HEAD_PAIR)])
        zg = _dot(u, w_in_ref[:, pl.ds(COL_G + off, HEAD_PAIR)])
        lb = lb_ref[:, pl.ds(off, HEAD_PAIR)]
        f = lb + (1.0 - lb) * _sigmoid(zf)
        log_f = jnp.log(f)
        k = 1.0 - f
        q = _silu(zq)
        lf_hi = log_f.astype(_BF16)
        lf_lo = (log_f - lf_hi.astype(_F32)).astype(_BF16)
        cum = _dot(tri, lf_hi) + _dot(tri, lf_lo)

        for c in range(MIX_TM // CHUNK):
            r0 = c * CHUNK
            outs = []
            for hh in range(2):
                l0 = hh * HG_DK
                hd = 2 * p + hh
                sl = (slice(r0, r0 + CHUNK), slice(l0, l0 + HG_DK))
                o, st_new = _hgrn2_chunk_head(q[sl], k[sl], cum[sl], zi[sl], zg[sl],
                                              st_ref[hd], norm_w)
                st_ref[hd] = st_new
                outs.append(o)
            hg_ref[pl.ds(r0, CHUNK), pl.ds(off, HEAD_PAIR)] = (
                jnp.concatenate(outs, axis=1).astype(_BF16))
        return carry

    lax.fori_loop(0, HG_HEADS // 2, pair_body, 0)

    u = u_ref[...]
    zv = _dot(u, w_in_ref[:, COL_V:COL_V + D_MODEL])
    gv = (_layer_norm(_gelu(zv)) * lnw_ref[...] + lnb_ref[...]).astype(_BF16)
    zu = _dot(u, w_in_ref[:, COL_U:COL_U + D_MODEL])
    gu = _gelu(zu)
    for nb in range(MIX_TM // GM_BLOCK):
        rows = slice(nb * GM_BLOCK, (nb + 1) * GM_BLOCK)
        for g in range(GM_GROUPS):
            cols = slice(g * GM_CG, (g + 1) * GM_CG)
            sv = _dot(ws_ref[g], gv[rows, cols]) + bs_ref[g]
            gm_ref[rows, cols] = (gu[rows, cols] * sv).astype(_BF16)

    y_a = _dot(hg_ref[...], wpa_ref[...])
    gate_a = _sigmoid(_dot(u, w_in_ref[:, COL_GA:COL_GA + D_MODEL]) + bgate_ref[0:1])
    mix = gate_a * y_a
    y_b = _dot(gm_ref[...], wpb_ref[...])
    gate_b = _sigmoid(_dot(u, w_in_ref[:, COL_GB:COL_GB + D_MODEL]) + bgate_ref[1:2])
    mix = (mix + gate_b * y_b).astype(_BF16)
    m = _dot(mix, wout_ref[...])
    out_ref[0] = _layer_norm(DEEPNORM_ALPHA * h + g1 * m) * ln1w_ref[...] + ln1b_ref[...]


def _const_spec(shape):
    return pl.BlockSpec(shape, lambda b, j: (0,) * len(shape), pipeline_mode=pl.Buffered(1))


def _mixer(x, mod, lb, w_in, tri, b_gate, norm_w, w_proj_a, ln_w, ln_b, ws, bs, w_proj_b, w_out,
           ln1_w, ln1_b):
    b, s, d = x.shape
    consts = (lb, w_in, tri, b_gate, norm_w, w_proj_a, ln_w, ln_b, ws, bs, w_proj_b, w_out,
              ln1_w, ln1_b)
    return pl.pallas_call(
        _mixer_kernel,
        grid=(b, s // MIX_TM),
        in_specs=[
            pl.BlockSpec((1, MIX_TM, d), lambda bi, j: (bi, j, 0)),
            pl.BlockSpec((1, 6, d), lambda bi, j: (bi, 0, 0)),
        ] + [_const_spec(a.shape) for a in consts],
        out_specs=pl.BlockSpec((1, MIX_TM, d), lambda bi, j: (bi, j, 0)),
        out_shape=jax.ShapeDtypeStruct((b, s, d), _F32),
        scratch_shapes=[
            pltpu.VMEM((HG_HEADS, HG_DK, HG_DK), _F32),
            pltpu.VMEM((MIX_TM, d), _BF16),
            pltpu.VMEM((MIX_TM, d), _BF16),
            pltpu.VMEM((MIX_TM, d), _BF16),
        ],
        compiler_params=pltpu.CompilerParams(
            dimension_semantics=("arbitrary", "arbitrary"),
            vmem_limit_bytes=VMEM_LIMIT),
        name="mixer",
    )(x, mod, *consts)


def _ffn_kernel(h_ref, mod_ref, w1_ref, w2_ref, ln2w_ref, ln2b_ref, out_ref):
    h = h_ref[0]
    sh2, sc2, g2 = mod_ref[0, 3:4], mod_ref[0, 4:5], mod_ref[0, 5:6]
    u = (_layer_norm(h) * (1.0 + sc2) + sh2).astype(_BF16)
    a = _dot(u, w1_ref[:, 0:D_FF])
    bb = _dot(u, w1_ref[:, D_FF:2 * D_FF])
    act = (_silu(a) * bb).astype(_BF16)
    ffn = _dot(act, w2_ref[...])
    out_ref[0] = _layer_norm(DEEPNORM_ALPHA * h + g2 * ffn) * ln2w_ref[...] + ln2b_ref[...]


def _ffn(h, mod, w1, w2, ln2_w, ln2_b):
    b, s, d = h.shape
    consts = (w1, w2, ln2_w, ln2_b)
    return pl.pallas_call(
        _ffn_kernel,
        grid=(b, s // FFN_TM),
        in_specs=[
            pl.BlockSpec((1, FFN_TM, d), lambda bi, j: (bi, j, 0)),
            pl.BlockSpec((1, 6, d), lambda bi, j: (bi, 0, 0)),
        ] + [_const_spec(a.shape) for a in consts],
        out_specs=pl.BlockSpec((1, FFN_TM, d), lambda bi, j: (bi, j, 0)),
        out_shape=jax.ShapeDtypeStruct((b, s, d), _F32),
        compiler_params=pltpu.CompilerParams(
            dimension_semantics=("arbitrary", "arbitrary"),
            vmem_limit_bytes=VMEM_LIMIT),
        name="ffn",
    )(h, mod, *consts)


def _chunk_tri(n):
    r = jnp.arange(n)[:, None]
    c = jnp.arange(n)[None, :]
    return ((r >= c) & (r // CHUNK == c // CHUNK)).astype(_BF16)


def kernel(x, c, w_ada, b_ada, w_in, b_gate, hgrn_lb_logits, hgrn_norm_w, w_proj_a, gmlp_ln_w,
           gmlp_ln_b, gmlp_ws, gmlp_bs, w_proj_b, w_out, ln1_w, ln1_b, w_ffn_in, w_ffn_out,
           ln2_w, ln2_b):
    assert w_ada.shape[0] == 1, "single-layer block"
    row = lambda a: a.reshape(1, -1)
    mod, lb = _prep(c, w_ada[0], b_ada[0], hgrn_lb_logits)
    mod = mod.transpose(1, 0, 2)

    pos = jnp.arange(GM_BLOCK) // CHUNK
    ws = jnp.where(pos[:, None] >= pos[None, :], gmlp_ws[0], 0.0).astype(_BF16)
    bs = jnp.broadcast_to(gmlp_bs[0][:, :, None], (GM_GROUPS, GM_BLOCK, GM_CG))

    h1 = _mixer(x, mod, lb, w_in[0].astype(_BF16), _chunk_tri(MIX_TM), b_gate[0],
                row(hgrn_norm_w[0]), w_proj_a[0].astype(_BF16), row(gmlp_ln_w[0]),
                row(gmlp_ln_b[0]), ws, bs, w_proj_b[0].astype(_BF16), w_out[0].astype(_BF16),
                row(ln1_w[0]), row(ln1_b[0]))
    return _ffn(h1, mod, w_ffn_in[0].astype(_BF16), w_ffn_out[0].astype(_BF16),
                row(ln2_w[0]), row(ln2_b[0]))
```

```python
import jax
import jax.numpy as jnp
from jax import lax
from jax.experimental import pallas as pl
from jax.experimental.pallas import tpu as pltpu

D_MODEL = 1024
CHUNK = 64
SUB_CHUNK = 16
N_SUB = CHUNK // SUB_CHUNK
HG_DK = 128
HG_HEADS = D_MODEL // HG_DK
GM_BLOCK = 128
GM_GROUPS = 8
GM_CG = D_MODEL // GM_GROUPS
D_FF = 2816
DEEPNORM_ALPHA = 2.0 ** 0.25
LN_EPS = 1e-5
RMS_EPS = 1e-6

COL_Q, COL_F, COL_I, COL_G, COL_U, COL_V, COL_GA, COL_GB = (i * D_MODEL for i in range(8))

MIX_TM = 256
FFN_TM = 512
HEAD_PAIR = 2 * HG_DK
N_CHUNKS = MIX_TM // CHUNK
N_PAIRS = HG_HEADS // 2
VMEM_LIMIT = 60 * 1024 * 1024

_F32 = jnp.float32
_BF16 = jnp.bfloat16


def _dot(a, b):
    return jnp.dot(a, b, preferred_element_type=_F32)


def _dot_nt(a, b):
    return lax.dot_general(a, b, (((1,), (1,)), ((), ())), preferred_element_type=_F32)


def _dot_tn(a, b):
    return lax.dot_general(a, b, (((0,), (0,)), ((), ())), preferred_element_type=_F32)


def _layer_norm(x):
    xc = x - jnp.mean(x, axis=-1, keepdims=True)
    return xc * lax.rsqrt(jnp.mean(xc * xc, axis=-1, keepdims=True) + LN_EPS)


def _sigmoid(x):
    return 1.0 / (1.0 + jnp.exp(-x))


def _silu(x):
    return x * _sigmoid(x)


def _gelu(x):
    return 0.5 * x * (1.0 + lax.erf(x * (2.0 ** -0.5)))


def _prep_kernel(c_ref, w_ref, b_ref, lbl_ref, mod_ref, lb_ref):
    cond = _silu(c_ref[...])
    mod_ref[0] = jnp.dot(cond, w_ref[...], preferred_element_type=_F32,
                         precision=lax.Precision.HIGHEST) + b_ref[0]
    logits = lbl_ref[...]
    e = jnp.exp(logits - jnp.max(logits, axis=0, keepdims=True))
    lb_ref[...] = e[0:1] / jnp.sum(e, axis=0, keepdims=True)


def _prep(c, w_ada, b_ada, lb_logits):
    b = c.shape[0]
    return pl.pallas_call(
        _prep_kernel,
        grid=(6,),
        in_specs=[
            pl.BlockSpec((b, D_MODEL), lambda j: (0, 0)),
            pl.BlockSpec((D_MODEL, D_MODEL), lambda j: (0, j)),
            pl.BlockSpec((1, 1, D_MODEL), lambda j: (j, 0, 0)),
            pl.BlockSpec(lb_logits.shape, lambda j: (0, 0)),
        ],
        out_specs=[
            pl.BlockSpec((1, b, D_MODEL), lambda j: (j, 0, 0)),
            pl.BlockSpec((1, D_MODEL), lambda j: (0, 0)),
        ],
        out_shape=[
            jax.ShapeDtypeStruct((6, b, D_MODEL), _F32),
            jax.ShapeDtypeStruct((1, D_MODEL), _F32),
        ],
        name="prep",
    )(c, w_ada, b_ada.reshape(6, 1, D_MODEL), lb_logits)


def _pair_project(u, w_in_ref, p):
    off = p * HEAD_PAIR
    return tuple(_dot(u, w_in_ref[:, col + off:col + off + HEAD_PAIR])
                 for col in (COL_Q, COL_F, COL_I, COL_G))


def _pair_gates(zq, zf, lb, tri):
    f = lb + (1.0 - lb) * _sigmoid(zf)
    l2f = jnp.log2(f)
    hi = l2f.astype(_BF16)
    lo = (l2f - hi.astype(_F32)).astype(_BF16)
    cum = _dot(tri, hi) + _dot(tri, lo)
    return _silu(zq), 1.0 - f, cum


def _chunk_head_scores(q, k, cum, v):
    cl = cum[CHUNK - 1:CHUNK]
    v_b = v.astype(_BF16)
    qd = (q * jnp.exp2(cum)).astype(_BF16)
    kd = (k * jnp.exp2(cl - cum)).astype(_BF16)
    upd = _dot_tn(v_b, kd)
    scores = []
    for i in range(N_SUB):
        lo, hi = i * SUB_CHUNK, (i + 1) * SUB_CHUNK
        if i == 0:
            qs = qd[lo:hi]
            ks = (k[0:hi] * jnp.exp2(-cum[0:hi])).astype(_BF16)
        else:
            ref = cum[lo - 1:lo]
            qs = (q[lo:hi] * jnp.exp2(cum[lo:hi] - ref)).astype(_BF16)
            ks = (k[0:hi] * jnp.exp2(ref - cum[0:hi])).astype(_BF16)
        scores.append(_dot_nt(qs, ks))
    return qd, v_b, jnp.exp2(cl), upd, scores


def _chunk_head_outputs(st, qd, v_b, scores):
    o_inter = _dot_nt(qd, st.astype(_BF16))
    outs = []
    for i, sc in enumerate(scores):
        lo, hi = i * SUB_CHUNK, (i + 1) * SUB_CHUNK
        row = lax.broadcasted_iota(jnp.int32, sc.shape, 0) + lo
        col = lax.broadcasted_iota(jnp.int32, sc.shape, 1)
        sc = jnp.where(col <= row, sc, 0.0).astype(_BF16)
        outs.append(_dot(sc, v_b[0:hi]))
    return o_inter + jnp.concatenate(outs, axis=0)


def _mixer_kernel(x_ref, mod_ref, lb_ref, w_in_ref, tri_ref, bgate_ref, normw_ref, wpa_ref,
                  lnw_ref, lnb_ref, ws_ref, bs_ref, wpb_ref, wout_ref, ln1w_ref, ln1b_ref,
                  out_ref, st_ref, hg_ref, gm_ref):
    @pl.when(pl.program_id(1) == 0)
    def _():
        st_ref[...] = jnp.zeros_like(st_ref)

    h = x_ref[0]
    sh1, sc1, g1 = mod_ref[0, 0:1], mod_ref[0, 1:2], mod_ref[0, 2:3]
    u = (_layer_norm(h) * (1.0 + sc1) + sh1).astype(_BF16)
    tri = tri_ref[...]
    norm_w = normw_ref[...]

    def chunk_head_slices(c, hh):
        return (slice(c * CHUNK, (c + 1) * CHUNK), slice(hh * HG_DK, (hh + 1) * HG_DK))

    def stage2(z, gates):
        q, k, cum = gates
        return [[_chunk_head_scores(q[sl], k[sl], cum[sl], z[2][sl])
                 for sl in (chunk_head_slices(c, hh) for hh in range(2))]
                for c in range(N_CHUNKS)]

    def stage3(p, s2):
        outs = []
        for hh in range(2):
            st = st_ref[2 * p + hh]
            col = []
            for c in range(N_CHUNKS):
                qd, v_b, dec, upd, scores = s2[c][hh]
                col.append(_chunk_head_outputs(st, qd, v_b, scores))
                st = st * dec + upd
            st_ref[2 * p + hh] = st
            outs.append(col)
        return outs

    def stage4(p, z, s3):
        for hh in range(2):
            hd = 2 * p + hh
            for c in range(N_CHUNKS):
                sl = chunk_head_slices(c, hh)
                o = s3[hh][c]
                o = o * lax.rsqrt(jnp.mean(o * o, axis=-1, keepdims=True) + RMS_EPS) * norm_w
                hg_ref[sl[0], hd * HG_DK:(hd + 1) * HG_DK] = (o * _silu(z[3][sl])).astype(_BF16)

    field = {}
    z, s1, s2, s3 = {}, {}, {}, {}
    fillers = (COL_V, COL_U, COL_GA, COL_GB)
    for r in range(N_PAIRS + len(fillers)):
        if r < N_PAIRS:
            z[r] = _pair_project(u, w_in_ref, r)
        else:
            col = fillers[r - N_PAIRS]
            field[col] = _dot(u, w_in_ref[:, col:col + D_MODEL])
        p = r - 1
        if 0 <= p < N_PAIRS:
            lb = lb_ref[:, p * HEAD_PAIR:(p + 1) * HEAD_PAIR]
            s1[p] = _pair_gates(z[p][0], z[p][1], lb, tri)
        p = r - 2
        if 0 <= p < N_PAIRS:
            s2[p] = stage2(z[p], s1.pop(p))
        p = r - 3
        if 0 <= p < N_PAIRS:
            s3[p] = stage3(p, s2.pop(p))
        p = r - 4
        if 0 <= p < N_PAIRS:
            stage4(p, z.pop(p), s3.pop(p))

    gv = (_layer_norm(_gelu(field[COL_V])) * lnw_ref[...] + lnb_ref[...]).astype(_BF16)
    gu = _gelu(field[COL_U])
    for nb in range(MIX_TM // GM_BLOCK):
        rows = slice(nb * GM_BLOCK, (nb + 1) * GM_BLOCK)
        for g in range(GM_GROUPS):
            cols = slice(g * GM_CG, (g + 1) * GM_CG)
            sv = _dot(ws_ref[g], gv[rows, cols]) + bs_ref[g]
            gm_ref[rows, cols] = (gu[rows, cols] * sv).astype(_BF16)

    y_a = _dot(hg_ref[...], wpa_ref[...])
    mix = _sigmoid(field[COL_GA] + bgate_ref[0:1]) * y_a
    y_b = _dot(gm_ref[...], wpb_ref[...])
    mix = (mix + _sigmoid(field[COL_GB] + bgate_ref[1:2]) * y_b).astype(_BF16)
    m = _dot(mix, wout_ref[...])
    out_ref[0] = _layer_norm(DEEPNORM_ALPHA * h + g1 * m) * ln1w_ref[...] + ln1b_ref[...]


def _const_spec(shape):
    return pl.BlockSpec(shape, lambda b, j: (0,) * len(shape), pipeline_mode=pl.Buffered(1))


def _mixer(x, mod, lb, w_in, tri, b_gate, norm_w, w_proj_a, ln_w, ln_b, ws, bs, w_proj_b, w_out,
           ln1_w, ln1_b):
    b, s, d = x.shape
    consts = (lb, w_in, tri, b_gate, norm_w, w_proj_a, ln_w, ln_b, ws, bs, w_proj_b, w_out,
              ln1_w, ln1_b)
    return pl.pallas_call(
        _mixer_kernel,
        grid=(b, s // MIX_TM),
        in_specs=[
            pl.BlockSpec((1, MIX_TM, d), lambda bi, j: (bi, j, 0)),
            pl.BlockSpec((1, 6, d), lambda bi, j: (bi, 0, 0)),
        ] + [_const_spec(a.shape) for a in consts],
        out_specs=pl.BlockSpec((1, MIX_TM, d), lambda bi, j: (bi, j, 0)),
        out_shape=jax.ShapeDtypeStruct((b, s, d), _F32),
        scratch_shapes=[
            pltpu.VMEM((HG_HEADS, HG_DK, HG_DK), _F32),
            pltpu.VMEM((MIX_TM, d), _BF16),
            pltpu.VMEM((MIX_TM, d), _BF16),
        ],
        compiler_params=pltpu.CompilerParams(
            dimension_semantics=("arbitrary", "arbitrary"),
            vmem_limit_bytes=VMEM_LIMIT),
        name="mixer",
    )(x, mod, *consts)


def _ffn_kernel(h_ref, mod_ref, w1_ref, w2_ref, ln2w_ref, ln2b_ref, out_ref):
    h = h_ref[0]
    sh2, sc2, g2 = mod_ref[0, 3:4], mod_ref[0, 4:5], mod_ref[0, 5:6]
    u = (_layer_norm(h) * (1.0 + sc2) + sh2).astype(_BF16)
    a = _dot(u, w1_ref[:, 0:D_FF])
    bb = _dot(u, w1_ref[:, D_FF:2 * D_FF])
    act = (_silu(a) * bb).astype(_BF16)
    ffn = _dot(act, w2_ref[...])
    out_ref[0] = _layer_norm(DEEPNORM_ALPHA * h + g2 * ffn) * ln2w_ref[...] + ln2b_ref[...]


def _ffn(h, mod, w1, w2, ln2_w, ln2_b):
    b, s, d = h.shape
    consts = (w1, w2, ln2_w, ln2_b)
    return pl.pallas_call(
        _ffn_kernel,
        grid=(b, s // FFN_TM),
        in_specs=[
            pl.BlockSpec((1, FFN_TM, d), lambda bi, j: (bi, j, 0)),
            pl.BlockSpec((1, 6, d), lambda bi, j: (bi, 0, 0)),
        ] + [_const_spec(a.shape) for a in consts],
        out_specs=pl.BlockSpec((1, FFN_TM, d), lambda bi, j: (bi, j, 0)),
        out_shape=jax.ShapeDtypeStruct((b, s, d), _F32),
        compiler_params=pltpu.CompilerParams(
            dimension_semantics=("arbitrary", "arbitrary"),
            vmem_limit_bytes=VMEM_LIMIT),
        name="ffn",
    )(h, mod, *consts)


def _chunk_tri(n):
    r = jnp.arange(n)[:, None]
    c = jnp.arange(n)[None, :]
    return ((r >= c) & (r // CHUNK == c // CHUNK)).astype(_BF16)


def kernel(x, c, w_ada, b_ada, w_in, b_gate, hgrn_lb_logits, hgrn_norm_w, w_proj_a, gmlp_ln_w,
           gmlp_ln_b, gmlp_ws, gmlp_bs, w_proj_b, w_out, ln1_w, ln1_b, w_ffn_in, w_ffn_out,
           ln2_w, ln2_b):
    assert w_ada.shape[0] == 1, "single-layer block"
    row = lambda a: a.reshape(1, -1)
    mod, lb = _prep(c, w_ada[0], b_ada[0], hgrn_lb_logits)
    mod = mod.transpose(1, 0, 2)

    pos = jnp.arange(GM_BLOCK) // CHUNK
    ws = jnp.where(pos[:, None] >= pos[None, :], gmlp_ws[0], 0.0).astype(_BF16)
    bs = jnp.broadcast_to(gmlp_bs[0][:, :, None], (GM_GROUPS, GM_BLOCK, GM_CG))

    h1 = _mixer(x, mod, lb, w_in[0].astype(_BF16), _chunk_tri(MIX_TM), b_gate[0],
                row(hgrn_norm_w[0]), w_proj_a[0].astype(_BF16), row(gmlp_ln_w[0]),
                row(gmlp_ln_b[0]), ws, bs, w_proj_b[0].astype(_BF16), w_out[0].astype(_BF16),
                row(ln1_w[0]), row(ln1_b[0]))
    return _ffn(h1, mod, w_ffn_in[0].astype(_BF16), w_ffn_out[0].astype(_BF16),
                row(ln2_w[0]), row(ln2_b[0]))
```

```python
import functools

import jax
import jax.numpy as jnp
from jax import lax
from jax.experimental import pallas as pl
from jax.experimental.pallas import tpu as pltpu

D_MODEL = 1024
CHUNK = 64
SUB_CHUNK = 16
N_SUB = CHUNK // SUB_CHUNK
HG_DK = 128
HG_HEADS = D_MODEL // HG_DK
GM_BLOCK = 128
GM_GROUPS = 8
GM_CG = D_MODEL // GM_GROUPS
D_FF = 2816
DEEPNORM_ALPHA = 2.0 ** 0.25
LN_EPS = 1e-5
RMS_EPS = 1e-6

COL_Q, COL_F, COL_I, COL_G, COL_U, COL_V, COL_GA, COL_GB = (i * D_MODEL for i in range(8))

TM = 256
MXU_N = 256
HEAD_PAIR = MXU_N
N_CHUNKS = TM // CHUNK
N_PAIRS = HG_HEADS // 2
FF_SLICES = D_FF // MXU_N
VMEM_LIMIT = 62 * 1024 * 1024

_F32 = jnp.float32
_BF16 = jnp.bfloat16


def _dot(a, b):
    return jnp.dot(a, b, preferred_element_type=_F32)


def _dot_nt(a, b):
    return lax.dot_general(a, b, (((1,), (1,)), ((), ())), preferred_element_type=_F32)


def _dot_tn(a, b):
    return lax.dot_general(a, b, (((0,), (0,)), ((), ())), preferred_element_type=_F32)


def _pack_weight(w):
    k, n = w.shape
    pairs = w.astype(_BF16).reshape(k // 2, 2, n).swapaxes(1, 2)
    return lax.bitcast_convert_type(pairs, jnp.uint32)


def _weight(packed):
    return pltpu.bitcast(packed, _BF16)


def _layer_norm(x):
    xc = x - jnp.mean(x, axis=-1, keepdims=True)
    return xc * lax.rsqrt(jnp.mean(xc * xc, axis=-1, keepdims=True) + LN_EPS)


def _sigmoid(x):
    return 1.0 / (1.0 + jnp.exp(-x))


def _silu(x):
    return x * _sigmoid(x)


def _gelu(x):
    return 0.5 * x * (1.0 + lax.erf(x * (2.0 ** -0.5)))


def _prep_kernel(c_ref, w_ref, b_ref, lbl_ref, mod_ref, lb_ref):
    cond = _silu(c_ref[...])
    mod_ref[0] = jnp.dot(cond, w_ref[...], preferred_element_type=_F32,
                         precision=lax.Precision.HIGHEST) + b_ref[0]
    logits = lbl_ref[...]
    e = jnp.exp(logits - jnp.max(logits, axis=0, keepdims=True))
    lb_ref[...] = e[0:1] / jnp.sum(e, axis=0, keepdims=True)


def _prep(c, w_ada, b_ada, lb_logits):
    b = c.shape[0]
    return pl.pallas_call(
        _prep_kernel,
        grid=(6,),
        in_specs=[
            pl.BlockSpec((b, D_MODEL), lambda j: (0, 0)),
            pl.BlockSpec((D_MODEL, D_MODEL), lambda j: (0, j)),
            pl.BlockSpec((1, 1, D_MODEL), lambda j: (j, 0, 0)),
            pl.BlockSpec(lb_logits.shape, lambda j: (0, 0)),
        ],
        out_specs=[
            pl.BlockSpec((1, b, D_MODEL), lambda j: (j, 0, 0)),
            pl.BlockSpec((1, D_MODEL), lambda j: (0, 0)),
        ],
        out_shape=[
            jax.ShapeDtypeStruct((6, b, D_MODEL), _F32),
            jax.ShapeDtypeStruct((1, D_MODEL), _F32),
        ],
        name="prep",
    )(c, w_ada, b_ada.reshape(6, 1, D_MODEL), lb_logits)


def _pair_project(u, w_in_ref, p):
    off = p * HEAD_PAIR
    return tuple(_dot(u, _weight(w_in_ref[:, col + off:col + off + HEAD_PAIR]))
                 for col in (COL_Q, COL_F, COL_I, COL_G))


def _pair_gates(zq, zf, lb, tri):
    f = lb + (1.0 - lb) * _sigmoid(zf)
    l2f = jnp.log2(f)
    hi = l2f.astype(_BF16)
    lo = (l2f - hi.astype(_F32)).astype(_BF16)
    cum = _dot(tri, hi) + _dot(tri, lo)
    return _silu(zq), 1.0 - f, cum


def _chunk_head_scores(q, k, cum, v):
    cl = cum[CHUNK - 1:CHUNK]
    v_b = v.astype(_BF16)
    qd = (q * jnp.exp2(cum)).astype(_BF16)
    kd = (k * jnp.exp2(cl - cum)).astype(_BF16)
    upd = _dot_tn(v_b, kd)
    scores = []
    for i in range(N_SUB):
        lo, hi = i * SUB_CHUNK, (i + 1) * SUB_CHUNK
        if i == 0:
            qs = qd[lo:hi]
            ks = (k[0:hi] * jnp.exp2(-cum[0:hi])).astype(_BF16)
        else:
            ref = cum[lo - 1:lo]
            qs = (q[lo:hi] * jnp.exp2(cum[lo:hi] - ref)).astype(_BF16)
            ks = (k[0:hi] * jnp.exp2(ref - cum[0:hi])).astype(_BF16)
        scores.append(_dot_nt(qs, ks))
    return qd, v_b, jnp.exp2(cl), upd, scores


def _chunk_head_outputs(st, qd, v_b, scores):
    o_inter = _dot_nt(qd, st.astype(_BF16))
    outs = []
    for i, sc in enumerate(scores):
        lo, hi = i * SUB_CHUNK, (i + 1) * SUB_CHUNK
        row = lax.broadcasted_iota(jnp.int32, sc.shape, 0) + lo
        col = lax.broadcasted_iota(jnp.int32, sc.shape, 1)
        sc = jnp.where(col <= row, sc, 0.0).astype(_BF16)
        outs.append(_dot(sc, v_b[0:hi]))
    return o_inter + jnp.concatenate(outs, axis=0)


def _block_kernel(n_seq_tiles,
                  x_ref, mod_ref, modf_ref, lb_ref, w_in_ref, tri_ref, bgate_ref, normw_ref,
                  wpa_ref, lnw_ref, lnb_ref, ws_ref, bs_ref, wpb_ref, wout_ref, ln1w_ref, ln1b_ref,
                  w1_ref, w2_ref, ln2w_ref, ln2b_ref,
                  out_ref, st_ref, hg_ref, gm_ref, h1_ref, u2_ref, act_ref):
    t = pl.program_id(0)

    @pl.when(t == 0)
    def _():
        h1_ref[...] = jnp.zeros_like(h1_ref)
        u2_ref[...] = jnp.zeros_like(u2_ref)

    @pl.when(t % n_seq_tiles == 0)
    def _():
        st_ref[...] = jnp.zeros_like(st_ref)

    cur, prev = t % 2, (t + 1) % 2
    half = TM // 2

    def ffn_slice(i):
        u2 = u2_ref[...]
        a = _dot(u2, _weight(w1_ref[:, i * MXU_N:(i + 1) * MXU_N]))
        b = _dot(u2, _weight(w1_ref[:, D_FF + i * MXU_N:D_FF + (i + 1) * MXU_N]))
        act_ref[:, i * MXU_N:(i + 1) * MXU_N] = (_silu(a) * b).astype(_BF16)

    def ffn_down(rows):
        return _dot(act_ref[rows, :], _weight(w2_ref[...]))

    def ffn_finish(rows, ffn):
        g2 = modf_ref[0, 5:6]
        r = DEEPNORM_ALPHA * h1_ref[prev, rows, :] + g2 * ffn
        out_ref[0, rows, :] = _layer_norm(r) * ln2w_ref[...] + ln2b_ref[...]

    sh1, sc1, g1 = mod_ref[0, 0:1], mod_ref[0, 1:2], mod_ref[0, 2:3]
    ffn_slice(0)
    ffn_slice(1)
    u = (_layer_norm(x_ref[0]) * (1.0 + sc1) + sh1).astype(_BF16)
    tri = tri_ref[...]
    norm_w = normw_ref[...]

    def chunk_head_slices(c, hh):
        return (slice(c * CHUNK, (c + 1) * CHUNK), slice(hh * HG_DK, (hh + 1) * HG_DK))

    def stage2(z, gates):
        q, k, cum = gates
        return [[_chunk_head_scores(q[sl], k[sl], cum[sl], z[2][sl])
                 for sl in (chunk_head_slices(c, hh) for hh in range(2))]
                for c in range(N_CHUNKS)]

    def stage3(p, s2):
        outs = []
        for hh in range(2):
            st = st_ref[2 * p + hh]
            col = []
            for c in range(N_CHUNKS):
                qd, v_b, dec, upd, scores = s2[c][hh]
                col.append(_chunk_head_outputs(st, qd, v_b, scores))
                st = st * dec + upd
            st_ref[2 * p + hh] = st
            outs.append(col)
        return outs

    def stage4(p, z, s3):
        for hh in range(2):
            hd = 2 * p + hh
            for c in range(N_CHUNKS):
                sl = chunk_head_slices(c, hh)
                o = s3[hh][c]
                o = o * lax.rsqrt(jnp.mean(o * o, axis=-1, keepdims=True) + RMS_EPS) * norm_w
                hg_ref[sl[0], hd * HG_DK:(hd + 1) * HG_DK] = (o * _silu(z[3][sl])).astype(_BF16)

    field = {}
    z, s1, s2, s3 = {}, {}, {}, {}
    fillers = (COL_V, COL_U, COL_GA, COL_GB)
    for r in range(N_PAIRS + len(fillers)):
        if r < N_PAIRS:
            z[r] = _pair_project(u, w_in_ref, r)
        else:
            col = fillers[r - N_PAIRS]
            field[col] = _dot(u, _weight(w_in_ref[:, col:col + D_MODEL]))
        ffn_slice(r + 2)
        p = r - 1
        if 0 <= p < N_PAIRS:
            lb = lb_ref[:, p * HEAD_PAIR:(p + 1) * HEAD_PAIR]
            s1[p] = _pair_gates(z[p][0], z[p][1], lb, tri)
        p = r - 2
        if 0 <= p < N_PAIRS:
            s2[p] = stage2(z[p], s1.pop(p))
        p = r - 3
        if 0 <= p < N_PAIRS:
            s3[p] = stage3(p, s2.pop(p))
        p = r - 4
        if 0 <= p < N_PAIRS:
            stage4(p, z.pop(p), s3.pop(p))
    ffn_slice(FF_SLICES - 1)

    gv = (_layer_norm(_gelu(field[COL_V])) * lnw_ref[...] + lnb_ref[...]).astype(_BF16)
    gu = _gelu(field[COL_U])
    for nb in range(TM // GM_BLOCK):
        rows = slice(nb * GM_BLOCK, (nb + 1) * GM_BLOCK)
        for g in range(GM_GROUPS):
            cols = slice(g * GM_CG, (g + 1) * GM_CG)
            sv = _dot(ws_ref[g], gv[rows, cols]) + bs_ref[g]
            gm_ref[rows, cols] = (gu[rows, cols] * sv).astype(_BF16)

    top, bot = slice(0, half), slice(half, TM)
    y_b = _dot(gm_ref[...], _weight(wpb_ref[...]))
    y_a = _dot(hg_ref[...], _weight(wpa_ref[...]))
    mix = (_sigmoid(field[COL_GA] + bgate_ref[0:1]) * y_a
           + _sigmoid(field[COL_GB] + bgate_ref[1:2]) * y_b).astype(_BF16)
    m = _dot(mix, _weight(wout_ref[...]))
    ffn_top = ffn_down(top)
    ffn_bot = ffn_down(bot)
    h1 = _layer_norm(DEEPNORM_ALPHA * x_ref[0] + g1 * m) * ln1w_ref[...] + ln1b_ref[...]
    h1_ref[cur] = h1
    sh2, sc2 = mod_ref[0, 3:4], mod_ref[0, 4:5]
    u2_new = (_layer_norm(h1) * (1.0 + sc2) + sh2).astype(_BF16)
    ffn_finish(top, ffn_top)
    ffn_finish(bot, ffn_bot)
    u2_ref[...] = u2_new


def _block(x, mod, consts):
    n_tiles, _, d = x.shape
    n_batch = mod.shape[0]
    n_seq_tiles = n_tiles // n_batch
    last = n_tiles - 1

    def const_spec(a):
        return pl.BlockSpec(a.shape, lambda t: (0,) * a.ndim, pipeline_mode=pl.Buffered(1))

    return pl.pallas_call(
        functools.partial(_block_kernel, n_seq_tiles),
        grid=(n_tiles + 1,),
        in_specs=[
            pl.BlockSpec((1, TM, d), lambda t: (jnp.minimum(t, last), 0, 0)),
            pl.BlockSpec((1, 6, d), lambda t: (jnp.minimum(t, last) // n_seq_tiles, 0, 0)),
            pl.BlockSpec((1, 6, d), lambda t: (jnp.maximum(t - 1, 0) // n_seq_tiles, 0, 0)),
        ] + [const_spec(a) for a in consts],
        out_specs=pl.BlockSpec((1, TM, d), lambda t: (jnp.maximum(t - 1, 0), 0, 0)),
        out_shape=jax.ShapeDtypeStruct((n_tiles, TM, d), _F32),
        scratch_shapes=[
            pltpu.VMEM((HG_HEADS, HG_DK, HG_DK), _F32),
            pltpu.VMEM((TM, d), _BF16),
            pltpu.VMEM((TM, d), _BF16),
            pltpu.VMEM((2, TM, d), _F32),
            pltpu.VMEM((TM, d), _BF16),
            pltpu.VMEM((TM, D_FF), _BF16),
        ],
        compiler_params=pltpu.CompilerParams(
            dimension_semantics=("arbitrary",),
            vmem_limit_bytes=VMEM_LIMIT),
        name="block",
    )(x, mod, mod, *consts)


def _chunk_tri(n):
    r = jnp.arange(n)[:, None]
    c = jnp.arange(n)[None, :]
    return ((r >= c) & (r // CHUNK == c // CHUNK)).astype(_BF16)


def kernel(x, c, w_ada, b_ada, w_in, b_gate, hgrn_lb_logits, hgrn_norm_w, w_proj_a, gmlp_ln_w,
           gmlp_ln_b, gmlp_ws, gmlp_bs, w_proj_b, w_out, ln1_w, ln1_b, w_ffn_in, w_ffn_out,
           ln2_w, ln2_b):
    assert w_ada.shape[0] == 1, "single-layer block"
    b, s, d = x.shape
    row = lambda a: a.reshape(1, -1)
    mod, lb = _prep(c, w_ada[0], b_ada[0], hgrn_lb_logits)
    mod = mod.transpose(1, 0, 2)

    pos = jnp.arange(GM_BLOCK) // CHUNK
    ws = jnp.where(pos[:, None] >= pos[None, :], gmlp_ws[0], 0.0).astype(_BF16)
    bs = jnp.broadcast_to(gmlp_bs[0][:, :, None], (GM_GROUPS, GM_BLOCK, GM_CG))

    consts = (lb, _pack_weight(w_in[0]), _chunk_tri(TM), b_gate[0], row(hgrn_norm_w[0]),
              _pack_weight(w_proj_a[0]), row(gmlp_ln_w[0]), row(gmlp_ln_b[0]), ws, bs,
              _pack_weight(w_proj_b[0]), _pack_weight(w_out[0]), row(ln1_w[0]), row(ln1_b[0]),
              _pack_weight(w_ffn_in[0]), _pack_weight(w_ffn_out[0]), row(ln2_w[0]), row(ln2_b[0]))
    out = _block(x.reshape(b * s // TM, TM, d), mod, consts)
    return out.reshape(b, s, d)
```

```python
import functools

import jax
import jax.numpy as jnp
from jax import lax
from jax.experimental import pallas as pl
from jax.experimental.pallas import tpu as pltpu

D_MODEL = 1024
CHUNK = 64
SUB_CHUNK = 16
N_SUB = CHUNK // SUB_CHUNK
HG_DK = 128
HG_HEADS = D_MODEL // HG_DK
GM_BLOCK = 128
GM_GROUPS = 8
GM_CG = D_MODEL // GM_GROUPS
D_FF = 2816
DEEPNORM_ALPHA = 2.0 ** 0.25
LN_EPS = 1e-5
RMS_EPS = 1e-6

COL_Q, COL_F, COL_I, COL_G, COL_U, COL_V, COL_GA, COL_GB = (i * D_MODEL for i in range(8))

TM = 256
MXU_N = 256
HEAD_PAIR = MXU_N
N_CHUNKS = TM // CHUNK
N_PAIRS = HG_HEADS // 2
FF_SLICES = D_FF // MXU_N
PACK_ROWS = 128
VMEM_LIMIT = 62 * 1024 * 1024

_F32 = jnp.float32
_BF16 = jnp.bfloat16


def _dot(a, b):
    return jnp.dot(a, b, preferred_element_type=_F32)


def _dot_nt(a, b):
    return lax.dot_general(a, b, (((1,), (1,)), ((), ())), preferred_element_type=_F32)


def _dot_tn(a, b):
    return lax.dot_general(a, b, (((0,), (0,)), ((), ())), preferred_element_type=_F32)


def _pack_kernel(w_ref, o_ref):
    o_ref[...] = pltpu.bitcast(w_ref[...].astype(_BF16), jnp.uint32)


def _pack_weight(w):
    k, n = w.shape
    return pl.pallas_call(
        _pack_kernel,
        grid=(k // PACK_ROWS,),
        in_specs=[pl.BlockSpec((PACK_ROWS, n), lambda i: (i, 0))],
        out_specs=pl.BlockSpec((PACK_ROWS // 2, n), lambda i: (i, 0)),
        out_shape=jax.ShapeDtypeStruct((k // 2, n), jnp.uint32),
        name="pack",
    )(w)


def _weight(packed):
    return pltpu.bitcast(packed, _BF16)


def _layer_norm(x):
    xc = x - jnp.mean(x, axis=-1, keepdims=True)
    return xc * lax.rsqrt(jnp.mean(xc * xc, axis=-1, keepdims=True) + LN_EPS)


def _sigmoid(x):
    return 1.0 / (1.0 + jnp.exp(-x))


def _silu(x):
    return x * _sigmoid(x)


def _gelu(x):
    return 0.5 * x * (1.0 + lax.erf(x * (2.0 ** -0.5)))


def _prep_kernel(c_ref, w_ref, b_ref, lbl_ref, mod_ref, lb_ref):
    cond = _silu(c_ref[...])
    mod_ref[0] = jnp.dot(cond, w_ref[...], preferred_element_type=_F32,
                         precision=lax.Precision.HIGHEST) + b_ref[0]
    logits = lbl_ref[...]
    e = jnp.exp(logits - jnp.max(logits, axis=0, keepdims=True))
    lb_ref[...] = e[0:1] / jnp.sum(e, axis=0, keepdims=True)


def _prep(c, w_ada, b_ada, lb_logits):
    b = c.shape[0]
    return pl.pallas_call(
        _prep_kernel,
        grid=(6,),
        in_specs=[
            pl.BlockSpec((b, D_MODEL), lambda j: (0, 0)),
            pl.BlockSpec((D_MODEL, D_MODEL), lambda j: (0, j)),
            pl.BlockSpec((1, 1, D_MODEL), lambda j: (j, 0, 0)),
            pl.BlockSpec(lb_logits.shape, lambda j: (0, 0)),
        ],
        out_specs=[
            pl.BlockSpec((1, b, D_MODEL), lambda j: (j, 0, 0)),
            pl.BlockSpec((1, D_MODEL), lambda j: (0, 0)),
        ],
        out_shape=[
            jax.ShapeDtypeStruct((6, b, D_MODEL), _F32),
            jax.ShapeDtypeStruct((1, D_MODEL), _F32),
        ],
        name="prep",
    )(c, w_ada, b_ada.reshape(6, 1, D_MODEL), lb_logits)


def _pair_project(u, w_in_ref, p):
    off = p * HEAD_PAIR
    return tuple(_dot(u, _weight(w_in_ref[:, col + off:col + off + HEAD_PAIR]))
                 for col in (COL_Q, COL_F, COL_I, COL_G))


def _pair_gates(zq, zf, lb, tri):
    f = lb + (1.0 - lb) * _sigmoid(zf)
    l2f = jnp.log2(f)
    hi = l2f.astype(_BF16)
    lo = (l2f - hi.astype(_F32)).astype(_BF16)
    cum = _dot(tri, hi) + _dot(tri, lo)
    return _silu(zq), 1.0 - f, cum


def _chunk_head_scores(q, k, cum, v):
    cl = cum[CHUNK - 1:CHUNK]
    v_b = v.astype(_BF16)
    qd = (q * jnp.exp2(cum)).astype(_BF16)
    kd = (k * jnp.exp2(cl - cum)).astype(_BF16)
    upd = _dot_tn(v_b, kd)
    scores = []
    for i in range(N_SUB):
        lo, hi = i * SUB_CHUNK, (i + 1) * SUB_CHUNK
        if i == 0:
            qs = qd[lo:hi]
            ks = (k[0:hi] * jnp.exp2(-cum[0:hi])).astype(_BF16)
        else:
            ref = cum[lo - 1:lo]
            qs = (q[lo:hi] * jnp.exp2(cum[lo:hi] - ref)).astype(_BF16)
            ks = (k[0:hi] * jnp.exp2(ref - cum[0:hi])).astype(_BF16)
        scores.append(_dot_nt(qs, ks))
    return qd, v_b, jnp.exp2(cl), upd, scores


def _chunk_head_outputs(st, qd, v_b, scores):
    o_inter = _dot_nt(qd, st.astype(_BF16))
    outs = []
    for i, sc in enumerate(scores):
        lo, hi = i * SUB_CHUNK, (i + 1) * SUB_CHUNK
        row = lax.broadcasted_iota(jnp.int32, sc.shape, 0) + lo
        col = lax.broadcasted_iota(jnp.int32, sc.shape, 1)
        sc = jnp.where(col <= row, sc, 0.0).astype(_BF16)
        outs.append(_dot(sc, v_b[0:hi]))
    return o_inter + jnp.concatenate(outs, axis=0)


def _block_kernel(n_seq_tiles,
                  x_ref, mod_ref, modf_ref, lb_ref, w_in_ref, tri_ref, bgate_ref, normw_ref,
                  wpa_ref, lnw_ref, lnb_ref, ws_ref, bs_ref, wpb_ref, wout_ref, ln1w_ref, ln1b_ref,
                  w1_ref, w2_ref, ln2w_ref, ln2b_ref,
                  out_ref, st_ref, hg_ref, gm_ref, h1_ref, u2_ref, act_ref):
    t = pl.program_id(0)

    @pl.when(t == 0)
    def _():
        h1_ref[...] = jnp.zeros_like(h1_ref)
        u2_ref[...] = jnp.zeros_like(u2_ref)

    @pl.when(t % n_seq_tiles == 0)
    def _():
        st_ref[...] = jnp.zeros_like(st_ref)

    cur, prev = t % 2, (t + 1) % 2
    half = TM // 2

    def ffn_slice(i):
        u2 = u2_ref[...]
        a = _dot(u2, _weight(w1_ref[:, i * MXU_N:(i + 1) * MXU_N]))
        b = _dot(u2, _weight(w1_ref[:, D_FF + i * MXU_N:D_FF + (i + 1) * MXU_N]))
        act_ref[:, i * MXU_N:(i + 1) * MXU_N] = (_silu(a) * b).astype(_BF16)

    def ffn_down(rows):
        return _dot(act_ref[rows, :], _weight(w2_ref[...]))

    def ffn_finish(rows, ffn):
        g2 = modf_ref[0, 5:6]
        r = DEEPNORM_ALPHA * h1_ref[prev, rows, :] + g2 * ffn
        out_ref[0, rows, :] = _layer_norm(r) * ln2w_ref[...] + ln2b_ref[...]

    sh1, sc1, g1 = mod_ref[0, 0:1], mod_ref[0, 1:2], mod_ref[0, 2:3]
    ffn_slice(0)
    ffn_slice(1)
    u = (_layer_norm(x_ref[0]) * (1.0 + sc1) + sh1).astype(_BF16)
    tri = tri_ref[...]
    norm_w = normw_ref[...]

    def chunk_head_slices(c, hh):
        return (slice(c * CHUNK, (c + 1) * CHUNK), slice(hh * HG_DK, (hh + 1) * HG_DK))

    def stage2(z, gates):
        q, k, cum = gates
        return [[_chunk_head_scores(q[sl], k[sl], cum[sl], z[2][sl])
                 for sl in (chunk_head_slices(c, hh) for hh in range(2))]
                for c in range(N_CHUNKS)]

    def stage3(p, s2):
        outs = []
        for hh in range(2):
            st = st_ref[2 * p + hh]
            col = []
            for c in range(N_CHUNKS):
                qd, v_b, dec, upd, scores = s2[c][hh]
                col.append(_chunk_head_outputs(st, qd, v_b, scores))
                st = st * dec + upd
            st_ref[2 * p + hh] = st
            outs.append(col)
        return outs

    def stage4(p, z, s3):
        for hh in range(2):
            hd = 2 * p + hh
            for c in range(N_CHUNKS):
                sl = chunk_head_slices(c, hh)
                o = s3[hh][c]
                o = o * lax.rsqrt(jnp.mean(o * o, axis=-1, keepdims=True) + RMS_EPS) * norm_w
                hg_ref[sl[0], hd * HG_DK:(hd + 1) * HG_DK] = (o * _silu(z[3][sl])).astype(_BF16)

    field = {}
    z, s1, s2, s3 = {}, {}, {}, {}
    fillers = (COL_V, COL_U, COL_GA, COL_GB)
    for r in range(N_PAIRS + len(fillers)):
        if r < N_PAIRS:
            z[r] = _pair_project(u, w_in_ref, r)
        else:
            col = fillers[r - N_PAIRS]
            field[col] = _dot(u, _weight(w_in_ref[:, col:col + D_MODEL]))
        ffn_slice(r + 2)
        p = r - 1
        if 0 <= p < N_PAIRS:
            lb = lb_ref[:, p * HEAD_PAIR:(p + 1) * HEAD_PAIR]
            s1[p] = _pair_gates(z[p][0], z[p][1], lb, tri)
        p = r - 2
        if 0 <= p < N_PAIRS:
            s2[p] = stage2(z[p], s1.pop(p))
        p = r - 3
        if 0 <= p < N_PAIRS:
            s3[p] = stage3(p, s2.pop(p))
        p = r - 4
        if 0 <= p < N_PAIRS:
            stage4(p, z.pop(p), s3.pop(p))
    ffn_slice(FF_SLICES - 1)

    gv = (_layer_norm(_gelu(field[COL_V])) * lnw_ref[...] + lnb_ref[...]).astype(_BF16)
    gu = _gelu(field[COL_U])
    for nb in range(TM // GM_BLOCK):
        rows = slice(nb * GM_BLOCK, (nb + 1) * GM_BLOCK)
        for g in range(GM_GROUPS):
            cols = slice(g * GM_CG, (g + 1) * GM_CG)
            sv = _dot(ws_ref[g], gv[rows, cols]) + bs_ref[g]
            gm_ref[rows, cols] = (gu[rows, cols] * sv).astype(_BF16)

    top, bot = slice(0, half), slice(half, TM)
    y_b = _dot(gm_ref[...], _weight(wpb_ref[...]))
    y_a = _dot(hg_ref[...], _weight(wpa_ref[...]))
    mix = (_sigmoid(field[COL_GA] + bgate_ref[0:1]) * y_a
           + _sigmoid(field[COL_GB] + bgate_ref[1:2]) * y_b).astype(_BF16)
    m = _dot(mix, _weight(wout_ref[...]))
    ffn_top = ffn_down(top)
    ffn_bot = ffn_down(bot)
    h1 = _layer_norm(DEEPNORM_ALPHA * x_ref[0] + g1 * m) * ln1w_ref[...] + ln1b_ref[...]
    h1_ref[cur] = h1
    sh2, sc2 = mod_ref[0, 3:4], mod_ref[0, 4:5]
    u2_new = (_layer_norm(h1) * (1.0 + sc2) + sh2).astype(_BF16)
    ffn_finish(top, ffn_top)
    ffn_finish(bot, ffn_bot)
    u2_ref[...] = u2_new


def _block(x, mod, consts):
    n_tiles, _, d = x.shape
    n_batch = mod.shape[0]
    n_seq_tiles = n_tiles // n_batch
    last = n_tiles - 1

    def const_spec(a):
        return pl.BlockSpec(a.shape, lambda t: (0,) * a.ndim, pipeline_mode=pl.Buffered(1))

    return pl.pallas_call(
        functools.partial(_block_kernel, n_seq_tiles),
        grid=(n_tiles + 1,),
        in_specs=[
            pl.BlockSpec((1, TM, d), lambda t: (jnp.minimum(t, last), 0, 0)),
            pl.BlockSpec((1, 6, d), lambda t: (jnp.minimum(t, last) // n_seq_tiles, 0, 0)),
            pl.BlockSpec((1, 6, d), lambda t: (jnp.maximum(t - 1, 0) // n_seq_tiles, 0, 0)),
        ] + [const_spec(a) for a in consts],
        out_specs=pl.BlockSpec((1, TM, d), lambda t: (jnp.maximum(t - 1, 0), 0, 0)),
        out_shape=jax.ShapeDtypeStruct((n_tiles, TM, d), _F32),
        scratch_shapes=[
            pltpu.VMEM((HG_HEADS, HG_DK, HG_DK), _F32),
            pltpu.VMEM((TM, d), _BF16),
            pltpu.VMEM((TM, d), _BF16),
            pltpu.VMEM((2, TM, d), _F32),
            pltpu.VMEM((TM, d), _BF16),
            pltpu.VMEM((TM, D_FF), _BF16),
        ],
        compiler_params=pltpu.CompilerParams(
            dimension_semantics=("arbitrary",),
            vmem_limit_bytes=VMEM_LIMIT),
        name="block",
    )(x, mod, mod, *consts)


def _chunk_tri(n):
    r = jnp.arange(n)[:, None]
    c = jnp.arange(n)[None, :]
    return ((r >= c) & (r // CHUNK == c // CHUNK)).astype(_BF16)


def kernel(x, c, w_ada, b_ada, w_in, b_gate, hgrn_lb_logits, hgrn_norm_w, w_proj_a, gmlp_ln_w,
           gmlp_ln_b, gmlp_ws, gmlp_bs, w_proj_b, w_out, ln1_w, ln1_b, w_ffn_in, w_ffn_out,
           ln2_w, ln2_b):
    assert w_ada.shape[0] == 1, "single-layer block"
    b, s, d = x.shape
    row = lambda a: a.reshape(1, -1)
    mod, lb = _prep(c, w_ada[0], b_ada[0], hgrn_lb_logits)
    mod = mod.transpose(1, 0, 2)

    pos = jnp.arange(GM_BLOCK) // CHUNK
    ws = jnp.where(pos[:, None] >= pos[None, :], gmlp_ws[0], 0.0).astype(_BF16)
    bs = jnp.broadcast_to(gmlp_bs[0][:, :, None], (GM_GROUPS, GM_BLOCK, GM_CG))

    consts = (lb, _pack_weight(w_in[0]), _chunk_tri(TM), b_gate[0], row(hgrn_norm_w[0]),
              _pack_weight(w_proj_a[0]), row(gmlp_ln_w[0]), row(gmlp_ln_b[0]), ws, bs,
              _pack_weight(w_proj_b[0]), _pack_weight(w_out[0]), row(ln1_w[0]), row(ln1_b[0]),
              _pack_weight(w_ffn_in[0]), _pack_weight(w_ffn_out[0]), row(ln2_w[0]), row(ln2_b[0]))
    out = _block(x.reshape(b * s // TM, TM, d), mod, consts)
    return out.reshape(b, s, d)
```

```python
import functools

import jax
import jax.numpy as jnp
from jax import lax
from jax.experimental import pallas as pl
from jax.experimental.pallas import tpu as pltpu

D_MODEL = 1024
CHUNK = 64
SUB_CHUNK = 16
N_SUB = CHUNK // SUB_CHUNK
HG_DK = 128
HG_HEADS = D_MODEL // HG_DK
GM_BLOCK = 128
GM_GROUPS = 8
GM_CG = D_MODEL // GM_GROUPS
D_FF = 2816
DEEPNORM_ALPHA = 2.0 ** 0.25
LN_EPS = 1e-5
RMS_EPS = 1e-6

COL_Q, COL_F, COL_I, COL_G, COL_U, COL_V, COL_GA, COL_GB = (i * D_MODEL for i in range(8))

TM = 256
MXU_N = 256
HEAD_PAIR = MXU_N
N_CHUNKS = TM // CHUNK
N_PAIRS = HG_HEADS // 2
FF_SLICES = D_FF // MXU_N
PACK_BLOCK_BYTES = 4 * 1024 * 1024
PACK_MIN_STEPS = 4
VMEM_LIMIT = 62 * 1024 * 1024

_F32 = jnp.float32
_BF16 = jnp.bfloat16


def _dot(a, b):
    return jnp.dot(a, b, preferred_element_type=_F32)


def _dot_nt(a, b):
    return lax.dot_general(a, b, (((1,), (1,)), ((), ())), preferred_element_type=_F32)


def _dot_tn(a, b):
    return lax.dot_general(a, b, (((0,), (0,)), ((), ())), preferred_element_type=_F32)


def _pack_kernel(w_ref, o_ref):
    o_ref[...] = pltpu.bitcast(w_ref[...].astype(_BF16), jnp.uint32)


def _pack_weight(w):
    k, n = w.shape
    steps = max(PACK_MIN_STEPS, k * n * 4 // PACK_BLOCK_BYTES)
    while k % (16 * steps):
        steps += 1
    rows = k // steps
    return pl.pallas_call(
        _pack_kernel,
        grid=(steps,),
        in_specs=[pl.BlockSpec((rows, n), lambda i: (i, 0))],
        out_specs=pl.BlockSpec((rows // 2, n), lambda i: (i, 0)),
        out_shape=jax.ShapeDtypeStruct((k // 2, n), jnp.uint32),
        name="pack",
    )(w)


def _weight(packed):
    return pltpu.bitcast(packed, _BF16)


def _layer_norm(x):
    xc = x - jnp.mean(x, axis=-1, keepdims=True)
    return xc * lax.rsqrt(jnp.mean(xc * xc, axis=-1, keepdims=True) + LN_EPS)


def _sigmoid(x):
    return 1.0 / (1.0 + jnp.exp(-x))


def _silu(x):
    return x * _sigmoid(x)


def _gelu(x):
    return 0.5 * x * (1.0 + lax.erf(x * (2.0 ** -0.5)))


def _prep_kernel(c_ref, w_ref, b_ref, lbl_ref, mod_ref, lb_ref):
    cond = _silu(c_ref[...])
    mod_ref[0] = jnp.dot(cond, w_ref[...], preferred_element_type=_F32,
                         precision=lax.Precision.HIGHEST) + b_ref[0]
    logits = lbl_ref[...]
    e = jnp.exp(logits - jnp.max(logits, axis=0, keepdims=True))
    lb_ref[...] = e[0:1] / jnp.sum(e, axis=0, keepdims=True)


def _prep(c, w_ada, b_ada, lb_logits):
    b = c.shape[0]
    return pl.pallas_call(
        _prep_kernel,
        grid=(6,),
        in_specs=[
            pl.BlockSpec((b, D_MODEL), lambda j: (0, 0)),
            pl.BlockSpec((D_MODEL, D_MODEL), lambda j: (0, j)),
            pl.BlockSpec((1, 1, D_MODEL), lambda j: (j, 0, 0)),
            pl.BlockSpec(lb_logits.shape, lambda j: (0, 0)),
        ],
        out_specs=[
            pl.BlockSpec((1, b, D_MODEL), lambda j: (j, 0, 0)),
            pl.BlockSpec((1, D_MODEL), lambda j: (0, 0)),
        ],
        out_shape=[
            jax.ShapeDtypeStruct((6, b, D_MODEL), _F32),
            jax.ShapeDtypeStruct((1, D_MODEL), _F32),
        ],
        name="prep",
    )(c, w_ada, b_ada.reshape(6, 1, D_MODEL), lb_logits)


def _pair_project(u, w_in_ref, p):
    off = p * HEAD_PAIR
    return tuple(_dot(u, _weight(w_in_ref[:, col + off:col + off + HEAD_PAIR]))
                 for col in (COL_Q, COL_F, COL_I, COL_G))


def _pair_gates(zq, zf, lb, tri):
    f = lb + (1.0 - lb) * _sigmoid(zf)
    l2f = jnp.log2(f)
    hi = l2f.astype(_BF16)
    lo = (l2f - hi.astype(_F32)).astype(_BF16)
    cum = _dot(tri, hi) + _dot(tri, lo)
    return _silu(zq), 1.0 - f, cum


def _chunk_head_scores(q, k, cum, v):
    cl = cum[CHUNK - 1:CHUNK]
    v_b = v.astype(_BF16)
    qd = (q * jnp.exp2(cum)).astype(_BF16)
    kd = (k * jnp.exp2(cl - cum)).astype(_BF16)
    upd = _dot_tn(kd, v_b)
    dec = jnp.transpose(jnp.broadcast_to(jnp.exp2(cl), (HG_DK, HG_DK)))

    def keys(ref, hi):
        return (k[0:hi] * jnp.exp2(-cum[0:hi] if ref is None else ref - cum[0:hi])).astype(_BF16)

    def queries(ref, lo):
        return (q[lo:lo + SUB_CHUNK] * jnp.exp2(cum[lo:lo + SUB_CHUNK] - ref)).astype(_BF16)

    zeros = jnp.zeros((SUB_CHUNK, HG_DK), _BF16)
    scores = []
    for i0 in range(0, N_SUB, 2):
        lo0, lo1 = i0 * SUB_CHUNK, (i0 + 1) * SUB_CHUNK
        hi = lo1 + SUB_CHUNK
        ref0 = None if i0 == 0 else cum[lo0 - 1:lo0]
        ref1 = cum[lo1 - 1:lo1]
        q0 = qd[lo0:lo1] if i0 == 0 else queries(ref0, lo0)
        q1 = queries(ref1, lo1)
        lhs = jnp.concatenate([jnp.concatenate([q0, zeros], axis=1),
                               jnp.concatenate([zeros, q1], axis=1)], axis=0)
        rhs = jnp.concatenate([jnp.concatenate([keys(ref0, lo1), zeros], axis=0),
                               keys(ref1, hi)], axis=1)
        scores.append(_dot_nt(lhs, rhs))
    return qd, v_b, dec, upd, scores


def _chunk_head_outputs(st, qd, v_b, scores):
    o_inter = _dot(qd, st.astype(_BF16))
    outs = []
    for j, sc in enumerate(scores):
        lo = 2 * j * SUB_CHUNK
        row = lax.broadcasted_iota(jnp.int32, sc.shape, 0) + lo
        col = lax.broadcasted_iota(jnp.int32, sc.shape, 1)
        sc = jnp.where(col <= row, sc, 0.0).astype(_BF16)
        outs.append(_dot(sc, v_b[0:sc.shape[1]]))
    return o_inter + jnp.concatenate(outs, axis=0)


def _block_kernel(n_seq_tiles,
                  x_ref, mod_ref, modf_ref, lb_ref, w_in_ref, tri_ref, bgate_ref, normw_ref,
                  wpa_ref, lnw_ref, lnb_ref, ws_ref, bs_ref, wpb_ref, wout_ref, ln1w_ref, ln1b_ref,
                  w1_ref, w2_ref, ln2w_ref, ln2b_ref,
                  out_ref, st_ref, hg_ref, gm_ref, h1_ref, u2_ref, act_ref):
    t = pl.program_id(0)

    @pl.when(t == 0)
    def _():
        h1_ref[...] = jnp.zeros_like(h1_ref)
        u2_ref[...] = jnp.zeros_like(u2_ref)

    @pl.when(t % n_seq_tiles == 0)
    def _():
        st_ref[...] = jnp.zeros_like(st_ref)

    cur, prev = t % 2, (t + 1) % 2
    half = TM // 2

    def ffn_slice(i):
        u2 = u2_ref[...]
        a = _dot(u2, _weight(w1_ref[:, i * MXU_N:(i + 1) * MXU_N]))
        b = _dot(u2, _weight(w1_ref[:, D_FF + i * MXU_N:D_FF + (i + 1) * MXU_N]))
        act_ref[:, i * MXU_N:(i + 1) * MXU_N] = (_silu(a) * b).astype(_BF16)

    def ffn_down(rows):
        return _dot(act_ref[rows, :], _weight(w2_ref[...]))

    def ffn_finish(rows, ffn):
        g2 = modf_ref[0, 5:6]
        r = DEEPNORM_ALPHA * h1_ref[prev, rows, :] + g2 * ffn
        out_ref[0, rows, :] = _layer_norm(r) * ln2w_ref[...] + ln2b_ref[...]

    sh1, sc1, g1 = mod_ref[0, 0:1], mod_ref[0, 1:2], mod_ref[0, 2:3]
    ffn_slice(0)
    ffn_slice(1)
    u = (_layer_norm(x_ref[0]) * (1.0 + sc1) + sh1).astype(_BF16)
    tri = tri_ref[...]
    norm_w = normw_ref[...]

    def chunk_head_slices(c, hh):
        return (slice(c * CHUNK, (c + 1) * CHUNK), slice(hh * HG_DK, (hh + 1) * HG_DK))

    def stage2(z, gates):
        q, k, cum = gates
        return [[_chunk_head_scores(q[sl], k[sl], cum[sl], z[2][sl])
                 for sl in (chunk_head_slices(c, hh) for hh in range(2))]
                for c in range(N_CHUNKS)]

    def stage3(p, s2):
        outs = []
        for hh in range(2):
            st = st_ref[2 * p + hh]
            col = []
            for c in range(N_CHUNKS):
                qd, v_b, dec, upd, scores = s2[c][hh]
                col.append(_chunk_head_outputs(st, qd, v_b, scores))
                st = st * dec + upd
            st_ref[2 * p + hh] = st
            outs.append(col)
        return outs

    def stage4(p, z, s3):
        for hh in range(2):
            hd = 2 * p + hh
            for c in range(N_CHUNKS):
                sl = chunk_head_slices(c, hh)
                o = s3[hh][c]
                o = o * lax.rsqrt(jnp.mean(o * o, axis=-1, keepdims=True) + RMS_EPS) * norm_w
                hg_ref[sl[0], hd * HG_DK:(hd + 1) * HG_DK] = (o * _silu(z[3][sl])).astype(_BF16)

    field = {}
    z, s1, s2, s3 = {}, {}, {}, {}
    fillers = (COL_V, COL_U, COL_GA, COL_GB)
    for r in range(N_PAIRS + len(fillers)):
        if r < N_PAIRS:
            z[r] = _pair_project(u, w_in_ref, r)
        else:
            col = fillers[r - N_PAIRS]
            field[col] = _dot(u, _weight(w_in_ref[:, col:col + D_MODEL]))
        ffn_slice(r + 2)
        p = r - 1
        if 0 <= p < N_PAIRS:
            lb = lb_ref[:, p * HEAD_PAIR:(p + 1) * HEAD_PAIR]
            s1[p] = _pair_gates(z[p][0], z[p][1], lb, tri)
        p = r - 2
        if 0 <= p < N_PAIRS:
            s2[p] = stage2(z[p], s1.pop(p))
        p = r - 3
        if 0 <= p < N_PAIRS:
            s3[p] = stage3(p, s2.pop(p))
        p = r - 4
        if 0 <= p < N_PAIRS:
            stage4(p, z.pop(p), s3.pop(p))
    ffn_slice(FF_SLICES - 1)

    gv = (_layer_norm(_gelu(field[COL_V])) * lnw_ref[...] + lnb_ref[...]).astype(_BF16)
    gu = _gelu(field[COL_U])
    for nb in range(TM // GM_BLOCK):
        rows = slice(nb * GM_BLOCK, (nb + 1) * GM_BLOCK)
        for g in range(GM_GROUPS):
            cols = slice(g * GM_CG, (g + 1) * GM_CG)
            sv = _dot(ws_ref[g], gv[rows, cols]) + bs_ref[g]
            gm_ref[rows, cols] = (gu[rows, cols] * sv).astype(_BF16)

    top, bot = slice(0, half), slice(half, TM)
    y_b = _dot(gm_ref[...], _weight(wpb_ref[...]))
    y_a = _dot(hg_ref[...], _weight(wpa_ref[...]))
    mix = (_sigmoid(field[COL_GA] + bgate_ref[0:1]) * y_a
           + _sigmoid(field[COL_GB] + bgate_ref[1:2]) * y_b).astype(_BF16)
    m = _dot(mix, _weight(wout_ref[...]))
    ffn_top = ffn_down(top)
    ffn_bot = ffn_down(bot)
    h1 = _layer_norm(DEEPNORM_ALPHA * x_ref[0] + g1 * m) * ln1w_ref[...] + ln1b_ref[...]
    h1_ref[cur] = h1
    sh2, sc2 = mod_ref[0, 3:4], mod_ref[0, 4:5]
    u2_new = (_layer_norm(h1) * (1.0 + sc2) + sh2).astype(_BF16)
    ffn_finish(top, ffn_top)
    ffn_finish(bot, ffn_bot)
    u2_ref[...] = u2_new


def _block(x, mod, consts):
    n_tiles, _, d = x.shape
    n_batch = mod.shape[0]
    n_seq_tiles = n_tiles // n_batch
    last = n_tiles - 1

    def const_spec(a):
        return pl.BlockSpec(a.shape, lambda t: (0,) * a.ndim, pipeline_mode=pl.Buffered(1))

    return pl.pallas_call(
        functools.partial(_block_kernel, n_seq_tiles),
        grid=(n_tiles + 1,),
        in_specs=[
            pl.BlockSpec((1, TM, d), lambda t: (jnp.minimum(t, last), 0, 0)),
            pl.BlockSpec((1, 6, d), lambda t: (jnp.minimum(t, last) // n_seq_tiles, 0, 0)),
            pl.BlockSpec((1, 6, d), lambda t: (jnp.maximum(t - 1, 0) // n_seq_tiles, 0, 0)),
        ] + [const_spec(a) for a in consts],
        out_specs=pl.BlockSpec((1, TM, d), lambda t: (jnp.maximum(t - 1, 0), 0, 0)),
        out_shape=jax.ShapeDtypeStruct((n_tiles, TM, d), _F32),
        scratch_shapes=[
            pltpu.VMEM((HG_HEADS, HG_DK, HG_DK), _F32),
            pltpu.VMEM((TM, d), _BF16),
            pltpu.VMEM((TM, d), _BF16),
            pltpu.VMEM((2, TM, d), _F32),
            pltpu.VMEM((TM, d), _BF16),
            pltpu.VMEM((TM, D_FF), _BF16),
        ],
        compiler_params=pltpu.CompilerParams(
            dimension_semantics=("arbitrary",),
            vmem_limit_bytes=VMEM_LIMIT),
        name="block",
    )(x, mod, mod, *consts)


def _chunk_tri(n):
    r = jnp.arange(n)[:, None]
    c = jnp.arange(n)[None, :]
    return ((r >= c) & (r // CHUNK == c // CHUNK)).astype(_BF16)


def kernel(x, c, w_ada, b_ada, w_in, b_gate, hgrn_lb_logits, hgrn_norm_w, w_proj_a, gmlp_ln_w,
           gmlp_ln_b, gmlp_ws, gmlp_bs, w_proj_b, w_out, ln1_w, ln1_b, w_ffn_in, w_ffn_out,
           ln2_w, ln2_b):
    assert w_ada.shape[0] == 1, "single-layer block"
    b, s, d = x.shape
    row = lambda a: a.reshape(1, -1)
    mod, lb = _prep(c, w_ada[0], b_ada[0], hgrn_lb_logits)
    mod = mod.transpose(1, 0, 2)

    pos = jnp.arange(GM_BLOCK) // CHUNK
    ws = jnp.where(pos[:, None] >= pos[None, :], gmlp_ws[0], 0.0).astype(_BF16)
    bs = jnp.broadcast_to(gmlp_bs[0][:, :, None], (GM_GROUPS, GM_BLOCK, GM_CG))

    consts = (lb, _pack_weight(w_in[0]), _chunk_tri(TM), b_gate[0], row(hgrn_norm_w[0]),
              _pack_weight(w_proj_a[0]), row(gmlp_ln_w[0]), row(gmlp_ln_b[0]), ws, bs,
              _pack_weight(w_proj_b[0]), _pack_weight(w_out[0]), row(ln1_w[0]), row(ln1_b[0]),
              _pack_weight(w_ffn_in[0]), _pack_weight(w_ffn_out[0]), row(ln2_w[0]), row(ln2_b[0]))
    out = _block(x.reshape(b * s // TM, TM, d), mod, consts)
    return out.reshape(b, s, d)
```

```python
import functools

import jax
import jax.numpy as jnp
from jax import lax
from jax.experimental import pallas as pl
from jax.experimental.pallas import tpu as pltpu

D_MODEL = 1024
CHUNK = 64
SUB_CHUNK = 16
N_SUB = CHUNK // SUB_CHUNK
HG_DK = 128
HG_HEADS = D_MODEL // HG_DK
GM_BLOCK = 128
GM_GROUPS = 8
GM_CG = D_MODEL // GM_GROUPS
D_FF = 2816
DEEPNORM_ALPHA = 2.0 ** 0.25
LN_EPS = 1e-5
RMS_EPS = 1e-6

COL_Q, COL_F, COL_I, COL_G, COL_U, COL_V, COL_GA, COL_GB = (i * D_MODEL for i in range(8))

TM = 256
MXU_N = 256
HEAD_PAIR = MXU_N
N_CHUNKS = TM // CHUNK
N_PAIRS = HG_HEADS // 2
FF_SLICES = D_FF // MXU_N
PACK_BLOCK_BYTES = 4 * 1024 * 1024
PACK_MIN_STEPS = 4
VMEM_LIMIT = 62 * 1024 * 1024

_F32 = jnp.float32
_BF16 = jnp.bfloat16


def _dot(a, b):
    return jnp.dot(a, b, preferred_element_type=_F32)


def _dot_nt(a, b):
    return lax.dot_general(a, b, (((1,), (1,)), ((), ())), preferred_element_type=_F32)


def _dot_tn(a, b):
    return lax.dot_general(a, b, (((0,), (0,)), ((), ())), preferred_element_type=_F32)


def _pack_kernel(w_ref, o_ref):
    o_ref[...] = pltpu.bitcast(w_ref[...].astype(_BF16), jnp.uint32)


def _pack_weight(w):
    k, n = w.shape
    steps = max(PACK_MIN_STEPS, k * n * 4 // PACK_BLOCK_BYTES)
    while k % (16 * steps):
        steps += 1
    rows = k // steps
    return pl.pallas_call(
        _pack_kernel,
        grid=(steps,),
        in_specs=[pl.BlockSpec((rows, n), lambda i: (i, 0))],
        out_specs=pl.BlockSpec((rows // 2, n), lambda i: (i, 0)),
        out_shape=jax.ShapeDtypeStruct((k // 2, n), jnp.uint32),
        name="pack",
    )(w)


def _weight(packed):
    return pltpu.bitcast(packed, _BF16)


def _layer_norm(x):
    xc = x - jnp.mean(x, axis=-1, keepdims=True)
    return xc * lax.rsqrt(jnp.mean(xc * xc, axis=-1, keepdims=True) + LN_EPS)


def _sigmoid(x):
    return 1.0 / (1.0 + jnp.exp(-x))


def _silu(x):
    return x * _sigmoid(x)


def _gelu(x):
    return 0.5 * x * (1.0 + lax.erf(x * (2.0 ** -0.5)))


def _prep_kernel(c_ref, w_ref, b_ref, lbl_ref, mod_ref, lb_ref):
    cond = _silu(c_ref[...])
    mod_ref[0] = jnp.dot(cond, w_ref[...], preferred_element_type=_F32,
                         precision=lax.Precision.HIGHEST) + b_ref[0]
    logits = lbl_ref[...]
    e = jnp.exp(logits - jnp.max(logits, axis=0, keepdims=True))
    lb_ref[...] = e[0:1] / jnp.sum(e, axis=0, keepdims=True)


def _prep(c, w_ada, b_ada, lb_logits):
    b = c.shape[0]
    return pl.pallas_call(
        _prep_kernel,
        grid=(6,),
        in_specs=[
            pl.BlockSpec((b, D_MODEL), lambda j: (0, 0)),
            pl.BlockSpec((D_MODEL, D_MODEL), lambda j: (0, j)),
            pl.BlockSpec((1, 1, D_MODEL), lambda j: (j, 0, 0)),
            pl.BlockSpec(lb_logits.shape, lambda j: (0, 0)),
        ],
        out_specs=[
            pl.BlockSpec((1, b, D_MODEL), lambda j: (j, 0, 0)),
            pl.BlockSpec((1, D_MODEL), lambda j: (0, 0)),
        ],
        out_shape=[
            jax.ShapeDtypeStruct((6, b, D_MODEL), _F32),
            jax.ShapeDtypeStruct((1, D_MODEL), _F32),
        ],
        name="prep",
    )(c, w_ada, b_ada.reshape(6, 1, D_MODEL), lb_logits)


def _pair_project(u, w_in_ref, p):
    off = p * HEAD_PAIR
    return tuple(_dot(u, _weight(w_in_ref[:, col + off:col + off + HEAD_PAIR]))
                 for col in (COL_Q, COL_F, COL_I, COL_G))


def _pair_gates(zq, zf, lb, tri):
    f = lb + (1.0 - lb) * _sigmoid(zf)
    l2f = jnp.log2(f)
    hi = l2f.astype(_BF16)
    lo = (l2f - hi.astype(_F32)).astype(_BF16)
    cum = _dot(tri, hi) + _dot(tri, lo)
    return _silu(zq), 1.0 - f, cum


def _chunk_head_operands(q, k, cum, v):
    cl = cum[CHUNK - 1:CHUNK]
    v_b = v.astype(_BF16)
    qd = (q * jnp.exp2(cum)).astype(_BF16)
    kd = (k * jnp.exp2(cl - cum)).astype(_BF16)
    dec = jnp.transpose(jnp.broadcast_to(jnp.exp2(cl), (HG_DK, HG_DK)))

    def keys(ref, hi):
        return (k[0:hi] * jnp.exp2(-cum[0:hi] if ref is None else ref - cum[0:hi])).astype(_BF16)

    def queries(ref, lo):
        return (q[lo:lo + SUB_CHUNK] * jnp.exp2(cum[lo:lo + SUB_CHUNK] - ref)).astype(_BF16)

    zeros = jnp.zeros((SUB_CHUNK, HG_DK), _BF16)
    scores = []
    for i0 in range(0, N_SUB, 2):
        lo0, lo1 = i0 * SUB_CHUNK, (i0 + 1) * SUB_CHUNK
        hi = lo1 + SUB_CHUNK
        ref0 = None if i0 == 0 else cum[lo0 - 1:lo0]
        ref1 = cum[lo1 - 1:lo1]
        q0 = qd[lo0:lo1] if i0 == 0 else queries(ref0, lo0)
        q1 = queries(ref1, lo1)
        lhs = jnp.concatenate([jnp.concatenate([q0, zeros], axis=1),
                               jnp.concatenate([zeros, q1], axis=1)], axis=0)
        rhs = jnp.concatenate([jnp.concatenate([keys(ref0, lo1), zeros], axis=0),
                               keys(ref1, hi)], axis=1)
        if hi < CHUNK:
            rhs = jnp.concatenate([rhs, jnp.zeros((CHUNK - hi, 2 * HG_DK), _BF16)], axis=0)
        scores.append((lhs, rhs))
    return qd, kd, v_b, dec, scores


def _chunk_head_score_dots(operands):
    qd, kd, v_b, dec, pairs = operands
    upd = _dot_tn(kd, v_b)
    return qd, v_b, dec, upd, [_dot_nt(lhs, rhs) for lhs, rhs in pairs]


def _chunk_head_outputs(st, qd, v_b, scores):
    sc = jnp.concatenate(scores, axis=0)
    row = lax.broadcasted_iota(jnp.int32, sc.shape, 0)
    col = lax.broadcasted_iota(jnp.int32, sc.shape, 1)
    sc = jnp.where(col <= row, sc, 0.0).astype(_BF16)
    return _dot(jnp.concatenate([qd, sc], axis=1),
                jnp.concatenate([st.astype(_BF16), v_b], axis=0))


def _block_kernel(n_seq_tiles,
                  x_ref, mod_ref, modf_ref, lb_ref, w_in_ref, tri_ref, bgate_ref, normw_ref,
                  wpa_ref, lnw_ref, lnb_ref, ws_ref, bs_ref, wpb_ref, wout_ref, ln1w_ref, ln1b_ref,
                  w1_ref, w2_ref, ln2w_ref, ln2b_ref,
                  out_ref, st_ref, hg_ref, gm_ref, h1_ref, u2_ref, act_ref):
    t = pl.program_id(0)

    @pl.when(t == 0)
    def _():
        h1_ref[...] = jnp.zeros_like(h1_ref)
        u2_ref[...] = jnp.zeros_like(u2_ref)

    @pl.when(t % n_seq_tiles == 0)
    def _():
        st_ref[...] = jnp.zeros_like(st_ref)

    cur, prev = t % 2, (t + 1) % 2
    half = TM // 2

    def ffn_slice(i):
        u2 = u2_ref[...]
        a = _dot(u2, _weight(w1_ref[:, i * MXU_N:(i + 1) * MXU_N]))
        b = _dot(u2, _weight(w1_ref[:, D_FF + i * MXU_N:D_FF + (i + 1) * MXU_N]))
        act_ref[:, i * MXU_N:(i + 1) * MXU_N] = (_silu(a) * b).astype(_BF16)

    def ffn_down(rows):
        return _dot(act_ref[rows, :], _weight(w2_ref[...]))

    def ffn_finish(rows, ffn):
        g2 = modf_ref[0, 5:6]
        r = DEEPNORM_ALPHA * h1_ref[prev, rows, :] + g2 * ffn
        out_ref[0, rows, :] = _layer_norm(r) * ln2w_ref[...] + ln2b_ref[...]

    sh1, sc1, g1 = mod_ref[0, 0:1], mod_ref[0, 1:2], mod_ref[0, 2:3]
    ffn_slice(0)
    ffn_slice(1)
    u = (_layer_norm(x_ref[0]) * (1.0 + sc1) + sh1).astype(_BF16)
    tri = tri_ref[...]
    norm_w = normw_ref[...]

    def chunk_head_slices(c, hh):
        return (slice(c * CHUNK, (c + 1) * CHUNK), slice(hh * HG_DK, (hh + 1) * HG_DK))

    def stage2_operands(z, gates):
        q, k, cum = gates
        return [[_chunk_head_operands(q[sl], k[sl], cum[sl], z[2][sl])
                 for sl in (chunk_head_slices(c, hh) for hh in range(2))]
                for c in range(N_CHUNKS)]

    def stage2_dots(operands):
        return [[_chunk_head_score_dots(o) for o in row] for row in operands]

    def stage3(p, s2):
        outs = []
        for hh in range(2):
            st = st_ref[2 * p + hh]
            col = []
            for c in range(N_CHUNKS):
                qd, v_b, dec, upd, scores = s2[c][hh]
                col.append(_chunk_head_outputs(st, qd, v_b, scores))
                st = st * dec + upd
            st_ref[2 * p + hh] = st
            outs.append(col)
        return outs

    def stage4(p, z, s3):
        for hh in range(2):
            hd = 2 * p + hh
            for c in range(N_CHUNKS):
                sl = chunk_head_slices(c, hh)
                o = s3[hh][c]
                o = o * lax.rsqrt(jnp.mean(o * o, axis=-1, keepdims=True) + RMS_EPS) * norm_w
                hg_ref[sl[0], hd * HG_DK:(hd + 1) * HG_DK] = (o * _silu(z[3][sl])).astype(_BF16)

    field = {}

    def spatial_gating():
        gv, gu = field[COL_V], field[COL_U]
        n_blocks = TM // GM_BLOCK
        for g in range(GM_GROUPS):
            cols = slice(g * GM_CG, (g + 1) * GM_CG)
            rhs = jnp.concatenate([gv[nb * GM_BLOCK:(nb + 1) * GM_BLOCK, cols]
                                   for nb in range(n_blocks)], axis=1)
            sv = _dot(ws_ref[g], rhs)
            for nb in range(n_blocks):
                rows = slice(nb * GM_BLOCK, (nb + 1) * GM_BLOCK)
                sv_nb = sv[:, nb * GM_CG:(nb + 1) * GM_CG] + bs_ref[g]
                gm_ref[rows, cols] = (gu[rows, cols] * sv_nb).astype(_BF16)

    consume = {
        COL_V: lambda zv: (_layer_norm(_gelu(zv)) * lnw_ref[...] + lnb_ref[...]).astype(_BF16),
        COL_U: _gelu,
        COL_GA: lambda zg: _sigmoid(zg + bgate_ref[0:1]),
        COL_GB: lambda zg: _sigmoid(zg + bgate_ref[1:2]),
    }

    z, s1, s2, s3, operands = {}, {}, {}, {}, {}
    fillers = (COL_V, COL_U, COL_GA, COL_GB)
    y_b = None
    for r in range(N_PAIRS + len(fillers)):
        p = r - 2
        if 0 <= p < N_PAIRS:
            operands[p] = stage2_operands(z[p], s1.pop(p))
        p = r - 1
        if 0 <= p < N_PAIRS:
            lb = lb_ref[:, p * HEAD_PAIR:(p + 1) * HEAD_PAIR]
            s1[p] = _pair_gates(z[p][0], z[p][1], lb, tri)
        if r < N_PAIRS:
            z[r] = _pair_project(u, w_in_ref, r)
        else:
            col = fillers[r - N_PAIRS]
            field[col] = consume[col](_dot(u, _weight(w_in_ref[:, col:col + D_MODEL])))
            if col == COL_GA:
                spatial_gating()
            if col == COL_GB:
                y_b = _dot(gm_ref[...], _weight(wpb_ref[...]))
        ffn_slice(r + 2)
        p = r - 2
        if 0 <= p < N_PAIRS:
            s2[p] = stage2_dots(operands.pop(p))
        p = r - 3
        if 0 <= p < N_PAIRS:
            s3[p] = stage3(p, s2.pop(p))
        p = r - 4
        if 0 <= p < N_PAIRS:
            stage4(p, z.pop(p), s3.pop(p))
    ffn_slice(FF_SLICES - 1)

    top, bot = slice(0, half), slice(half, TM)
    y_a = _dot(hg_ref[...], _weight(wpa_ref[...]))
    mix = (field[COL_GA] * y_a + field[COL_GB] * y_b).astype(_BF16)
    m = _dot(mix, _weight(wout_ref[...]))
    ffn_top = ffn_down(top)
    ffn_bot = ffn_down(bot)
    h1 = _layer_norm(DEEPNORM_ALPHA * x_ref[0] + g1 * m) * ln1w_ref[...] + ln1b_ref[...]
    h1_ref[cur] = h1
    sh2, sc2 = mod_ref[0, 3:4], mod_ref[0, 4:5]
    u2_new = (_layer_norm(h1) * (1.0 + sc2) + sh2).astype(_BF16)
    ffn_finish(top, ffn_top)
    ffn_finish(bot, ffn_bot)
    u2_ref[...] = u2_new


def _block(x, mod, consts):
    n_tiles, _, d = x.shape
    n_batch = mod.shape[0]
    n_seq_tiles = n_tiles // n_batch
    last = n_tiles - 1

    def const_spec(a):
        return pl.BlockSpec(a.shape, lambda t: (0,) * a.ndim, pipeline_mode=pl.Buffered(1))

    return pl.pallas_call(
        functools.partial(_block_kernel, n_seq_tiles),
        grid=(n_tiles + 1,),
        in_specs=[
            pl.BlockSpec((1, TM, d), lambda t: (jnp.minimum(t, last), 0, 0)),
            pl.BlockSpec((1, 6, d), lambda t: (jnp.minimum(t, last) // n_seq_tiles, 0, 0)),
            pl.BlockSpec((1, 6, d), lambda t: (jnp.maximum(t - 1, 0) // n_seq_tiles, 0, 0)),
        ] + [const_spec(a) for a in consts],
        out_specs=pl.BlockSpec((1, TM, d), lambda t: (jnp.maximum(t - 1, 0), 0, 0)),
        out_shape=jax.ShapeDtypeStruct((n_tiles, TM, d), _F32),
        scratch_shapes=[
            pltpu.VMEM((HG_HEADS, HG_DK, HG_DK), _F32),
            pltpu.VMEM((TM, d), _BF16),
            pltpu.VMEM((TM, d), _BF16),
            pltpu.VMEM((2, TM, d), _F32),
            pltpu.VMEM((TM, d), _BF16),
            pltpu.VMEM((TM, D_FF), _BF16),
        ],
        compiler_params=pltpu.CompilerParams(
            dimension_semantics=("arbitrary",),
            vmem_limit_bytes=VMEM_LIMIT),
        name="block",
    )(x, mod, mod, *consts)


def _chunk_tri(n):
    r = jnp.arange(n)[:, None]
    c = jnp.arange(n)[None, :]
    return ((r >= c) & (r // CHUNK == c // CHUNK)).astype(_BF16)


def kernel(x, c, w_ada, b_ada, w_in, b_gate, hgrn_lb_logits, hgrn_norm_w, w_proj_a, gmlp_ln_w,
           gmlp_ln_b, gmlp_ws, gmlp_bs, w_proj_b, w_out, ln1_w, ln1_b, w_ffn_in, w_ffn_out,
           ln2_w, ln2_b):
    assert w_ada.shape[0] == 1, "single-layer block"
    b, s, d = x.shape
    row = lambda a: a.reshape(1, -1)
    mod, lb = _prep(c, w_ada[0], b_ada[0], hgrn_lb_logits)
    mod = mod.transpose(1, 0, 2)

    pos = jnp.arange(GM_BLOCK) // CHUNK
    ws = jnp.where(pos[:, None] >= pos[None, :], gmlp_ws[0], 0.0).astype(_BF16)
    bs = jnp.broadcast_to(gmlp_bs[0][:, :, None], (GM_GROUPS, GM_BLOCK, GM_CG))

    consts = (lb, _pack_weight(w_in[0]), _chunk_tri(TM), b_gate[0], row(hgrn_norm_w[0]),
              _pack_weight(w_proj_a[0]), row(gmlp_ln_w[0]), row(gmlp_ln_b[0]), ws, bs,
              _pack_weight(w_proj_b[0]), _pack_weight(w_out[0]), row(ln1_w[0]), row(ln1_b[0]),
              _pack_weight(w_ffn_in[0]), _pack_weight(w_ffn_out[0]), row(ln2_w[0]), row(ln2_b[0]))
    out = _block(x.reshape(b * s // TM, TM, d), mod, consts)
    return out.reshape(b, s, d)
```

```python
import functools

import jax
import jax.numpy as jnp
from jax import lax
from jax.experimental import pallas as pl
from jax.experimental.pallas import tpu as pltpu

D_MODEL = 1024
CHUNK = 64
SUB_CHUNK = 16
N_SUB = CHUNK // SUB_CHUNK
HG_DK = 128
HG_HEADS = D_MODEL // HG_DK
GM_BLOCK = 128
GM_GROUPS = 8
GM_CG = D_MODEL // GM_GROUPS
D_FF = 2816
DEEPNORM_ALPHA = 2.0 ** 0.25
LN_EPS = 1e-5
RMS_EPS = 1e-6

COL_Q, COL_F, COL_I, COL_G, COL_U, COL_V, COL_GA, COL_GB = (i * D_MODEL for i in range(8))

TM = 256
MXU_N = 256
HEAD_PAIR = MXU_N
N_CHUNKS = TM // CHUNK
N_PAIRS = HG_HEADS // 2
FF_SLICES = D_FF // MXU_N
PACK_STEPS = 8
VMEM_LIMIT = 62 * 1024 * 1024

_F32 = jnp.float32
_BF16 = jnp.bfloat16


def _dot(a, b):
    return jnp.dot(a, b, preferred_element_type=_F32)


def _dot_nt(a, b):
    return lax.dot_general(a, b, (((1,), (1,)), ((), ())), preferred_element_type=_F32)


def _dot_tn(a, b):
    return lax.dot_general(a, b, (((0,), (0,)), ((), ())), preferred_element_type=_F32)


def _pack_kernel(*refs):
    n = len(refs) // 2
    for w_ref, o_ref in zip(refs[:n], refs[n:]):
        o_ref[...] = pltpu.bitcast(w_ref[...].astype(_BF16), jnp.uint32)


def _pack_weights(*ws):
    for w in ws:
        assert w.shape[0] % (16 * PACK_STEPS) == 0, w.shape
    rows = [w.shape[0] // PACK_STEPS for w in ws]
    return pl.pallas_call(
        _pack_kernel,
        grid=(PACK_STEPS,),
        in_specs=[pl.BlockSpec((r, w.shape[1]), lambda i: (i, 0)) for r, w in zip(rows, ws)],
        out_specs=[pl.BlockSpec((r // 2, w.shape[1]), lambda i: (i, 0)) for r, w in zip(rows, ws)],
        out_shape=[jax.ShapeDtypeStruct((w.shape[0] // 2, w.shape[1]), jnp.uint32) for w in ws],
        compiler_params=pltpu.CompilerParams(vmem_limit_bytes=VMEM_LIMIT),
        name="pack",
    )(*ws)


def _weight(packed):
    return pltpu.bitcast(packed, _BF16)


def _layer_norm(x):
    xc = x - jnp.mean(x, axis=-1, keepdims=True)
    return xc * lax.rsqrt(jnp.mean(xc * xc, axis=-1, keepdims=True) + LN_EPS)


def _sigmoid(x):
    return 1.0 / (1.0 + jnp.exp(-x))


def _silu(x):
    return x * _sigmoid(x)


def _gelu(x):
    return 0.5 * x * (1.0 + lax.erf(x * (2.0 ** -0.5)))


def _prep_kernel(c_ref, w_ref, b_ref, lbl_ref, mod_ref, lb_ref):
    cond = _silu(c_ref[...])
    mod_ref[0] = _dot(cond.astype(_BF16), w_ref[...].astype(_BF16)) + b_ref[0]
    logits = lbl_ref[...]
    e = jnp.exp(logits - jnp.max(logits, axis=0, keepdims=True))
    lb_ref[...] = e[0:1] / jnp.sum(e, axis=0, keepdims=True)


def _prep(c, w_ada, b_ada, lb_logits):
    b = c.shape[0]
    return pl.pallas_call(
        _prep_kernel,
        grid=(6,),
        in_specs=[
            pl.BlockSpec((b, D_MODEL), lambda j: (0, 0)),
            pl.BlockSpec((D_MODEL, D_MODEL), lambda j: (0, j)),
            pl.BlockSpec((1, 1, D_MODEL), lambda j: (j, 0, 0)),
            pl.BlockSpec(lb_logits.shape, lambda j: (0, 0)),
        ],
        out_specs=[
            pl.BlockSpec((1, b, D_MODEL), lambda j: (j, 0, 0)),
            pl.BlockSpec((1, D_MODEL), lambda j: (0, 0)),
        ],
        out_shape=[
            jax.ShapeDtypeStruct((6, b, D_MODEL), _F32),
            jax.ShapeDtypeStruct((1, D_MODEL), _F32),
        ],
        name="prep",
    )(c, w_ada, b_ada.reshape(6, 1, D_MODEL), lb_logits)


def _pair_project(u, w_in_ref, p):
    off = p * HEAD_PAIR
    zq, zf, zi, zg = (_dot(u, _weight(w_in_ref[:, col + off:col + off + HEAD_PAIR]))
                      for col in (COL_Q, COL_F, COL_I, COL_G))
    return zq, zf, zi.astype(_BF16), zg


def _pair_gates(zq, zf, lb, tri):
    f = lb + (1.0 - lb) * _sigmoid(zf)
    l2f = jnp.log2(f)
    hi = l2f.astype(_BF16)
    lo = (l2f - hi.astype(_F32)).astype(_BF16)
    cum = _dot(tri, hi) + _dot(tri, lo)
    return _silu(zq), 1.0 - f, cum


def _chunk_head_operands(q, k, cum, v_b):
    cl = cum[CHUNK - 1:CHUNK]
    qd = (q * jnp.exp2(cum)).astype(_BF16)
    kd = (k * jnp.exp2(cl - cum)).astype(_BF16)
    dec = jnp.transpose(jnp.broadcast_to(jnp.exp2(cl), (HG_DK, HG_DK)))

    def keys(ref, hi):
        return (k[0:hi] * jnp.exp2(-cum[0:hi] if ref is None else ref - cum[0:hi])).astype(_BF16)

    def queries(ref, lo):
        return (q[lo:lo + SUB_CHUNK] * jnp.exp2(cum[lo:lo + SUB_CHUNK] - ref)).astype(_BF16)

    zeros = jnp.zeros((SUB_CHUNK, HG_DK), _BF16)
    scores = []
    for i0 in range(0, N_SUB, 2):
        lo0, lo1 = i0 * SUB_CHUNK, (i0 + 1) * SUB_CHUNK
        hi = lo1 + SUB_CHUNK
        ref0 = None if i0 == 0 else cum[lo0 - 1:lo0]
        ref1 = cum[lo1 - 1:lo1]
        q0 = qd[lo0:lo1] if i0 == 0 else queries(ref0, lo0)
        q1 = queries(ref1, lo1)
        lhs = jnp.concatenate([jnp.concatenate([q0, zeros], axis=1),
                               jnp.concatenate([zeros, q1], axis=1)], axis=0)
        rhs = jnp.concatenate([jnp.concatenate([keys(ref0, lo1), zeros], axis=0),
                               keys(ref1, hi)], axis=1)
        scores.append((lhs, rhs))
    return qd, kd, v_b, dec, scores


def _chunk_head_score_dots(operands):
    qd, kd, v_b, dec, pairs = operands
    upd = _dot_tn(kd, v_b)
    return qd, v_b, dec, upd, [_dot_nt(lhs, rhs) for lhs, rhs in pairs]


def _chunk_head_outputs(st, qd, v_b, scores):
    o_inter = _dot(qd, st.astype(_BF16))
    outs = []
    for j, sc in enumerate(scores):
        lo = 2 * j * SUB_CHUNK
        row = lax.broadcasted_iota(jnp.int32, sc.shape, 0) + lo
        col = lax.broadcasted_iota(jnp.int32, sc.shape, 1)
        sc = jnp.where(col <= row, sc, 0.0).astype(_BF16)
        outs.append(_dot(sc, v_b[0:sc.shape[1]]))
    return o_inter + jnp.concatenate(outs, axis=0)


def _block_kernel(n_seq_tiles,
                  x_ref, mod_ref, modf_ref, lb_ref, w_in_ref, tri_ref, bgate_ref, normw_ref,
                  wpa_ref, lnw_ref, lnb_ref, ws_ref, bs_ref, wpb_ref, wout_ref, ln1w_ref, ln1b_ref,
                  w1_ref, w2_ref, ln2w_ref, ln2b_ref,
                  out_ref, st_ref, hg_ref, gm_ref, h1_ref, u2_ref, act_ref):
    t = pl.program_id(0)

    @pl.when(t == 0)
    def _():
        h1_ref[...] = jnp.zeros_like(h1_ref)
        u2_ref[...] = jnp.zeros_like(u2_ref)

    @pl.when(t % n_seq_tiles == 0)
    def _():
        st_ref[...] = jnp.zeros_like(st_ref)

    cur, prev = t % 2, (t + 1) % 2
    half = TM // 2

    def ffn_slice(i):
        u2 = u2_ref[...]
        a = _dot(u2, _weight(w1_ref[:, i * MXU_N:(i + 1) * MXU_N]))
        b = _dot(u2, _weight(w1_ref[:, D_FF + i * MXU_N:D_FF + (i + 1) * MXU_N]))
        act_ref[:, i * MXU_N:(i + 1) * MXU_N] = (_silu(a) * b).astype(_BF16)

    def ffn_down(rows):
        return _dot(act_ref[rows, :], _weight(w2_ref[...]))

    def ffn_finish(rows, ffn):
        g2 = modf_ref[0, 5:6]
        r = DEEPNORM_ALPHA * h1_ref[prev, rows, :] + g2 * ffn
        out_ref[0, rows, :] = _layer_norm(r) * ln2w_ref[...] + ln2b_ref[...]

    sh1, sc1, g1 = mod_ref[0, 0:1], mod_ref[0, 1:2], mod_ref[0, 2:3]
    ffn_slice(0)
    ffn_slice(1)
    u = (_layer_norm(x_ref[0]) * (1.0 + sc1) + sh1).astype(_BF16)
    tri = tri_ref[...]
    norm_w = normw_ref[...]

    def chunk_head_slices(c, hh):
        return (slice(c * CHUNK, (c + 1) * CHUNK), slice(hh * HG_DK, (hh + 1) * HG_DK))

    def stage2_operands(z, gates):
        q, k, cum = gates
        return [[_chunk_head_operands(q[sl], k[sl], cum[sl], z[2][sl])
                 for sl in (chunk_head_slices(c, hh) for hh in range(2))]
                for c in range(N_CHUNKS)]

    def stage2_dots(operands):
        return [[_chunk_head_score_dots(o) for o in row] for row in operands]

    def stage3(p, s2):
        outs = []
        for hh in range(2):
            st = st_ref[2 * p + hh]
            col = []
            for c in range(N_CHUNKS):
                qd, v_b, dec, upd, scores = s2[c][hh]
                col.append(_chunk_head_outputs(st, qd, v_b, scores))
                st = st * dec + upd
            st_ref[2 * p + hh] = st
            outs.append(col)
        return outs

    def stage4(p, z, s3):
        for hh in range(2):
            hd = 2 * p + hh
            for c in range(N_CHUNKS):
                sl = chunk_head_slices(c, hh)
                o = s3[hh][c]
                o = o * lax.rsqrt(jnp.mean(o * o, axis=-1, keepdims=True) + RMS_EPS) * norm_w
                hg_ref[sl[0], hd * HG_DK:(hd + 1) * HG_DK] = (o * _silu(z[3][sl])).astype(_BF16)

    field = {}

    def spatial_gating():
        gv, gu = field[COL_V], field[COL_U]
        n_blocks = TM // GM_BLOCK
        for g in range(GM_GROUPS):
            cols = slice(g * GM_CG, (g + 1) * GM_CG)
            rhs = jnp.concatenate([gv[nb * GM_BLOCK:(nb + 1) * GM_BLOCK, cols]
                                   for nb in range(n_blocks)], axis=1)
            sv = _dot(ws_ref[g], rhs)
            for nb in range(n_blocks):
                rows = slice(nb * GM_BLOCK, (nb + 1) * GM_BLOCK)
                sv_nb = sv[:, nb * GM_CG:(nb + 1) * GM_CG] + bs_ref[g]
                gm_ref[rows, cols] = (gu[rows, cols] * sv_nb).astype(_BF16)

    consume = {
        COL_V: lambda zv: (_layer_norm(_gelu(zv)) * lnw_ref[...] + lnb_ref[...]).astype(_BF16),
        COL_U: _gelu,
        COL_GA: lambda zg: _sigmoid(zg + bgate_ref[0:1]),
        COL_GB: lambda zg: _sigmoid(zg + bgate_ref[1:2]),
    }

    z, s1, s2, s3, operands = {}, {}, {}, {}, {}
    fillers = (COL_V, COL_U, COL_GA, COL_GB)
    y_b = None
    for r in range(N_PAIRS + len(fillers)):
        p = r - 2
        if 0 <= p < N_PAIRS:
            operands[p] = stage2_operands(z[p], s1.pop(p))
        p = r - 1
        if 0 <= p < N_PAIRS:
            lb = lb_ref[:, p * HEAD_PAIR:(p + 1) * HEAD_PAIR]
            s1[p] = _pair_gates(z[p][0], z[p][1], lb, tri)
        if r < N_PAIRS:
            z[r] = _pair_project(u, w_in_ref, r)
        else:
            col = fillers[r - N_PAIRS]
            field[col] = consume[col](_dot(u, _weight(w_in_ref[:, col:col + D_MODEL])))
            if col == COL_GA:
                spatial_gating()
            if col == COL_GB:
                y_b = _dot(gm_ref[...], _weight(wpb_ref[...]))
        ffn_slice(r + 2)
        p = r - 2
        if 0 <= p < N_PAIRS:
            s2[p] = stage2_dots(operands.pop(p))
        p = r - 3
        if 0 <= p < N_PAIRS:
            s3[p] = stage3(p, s2.pop(p))
        p = r - 4
        if 0 <= p < N_PAIRS:
            stage4(p, z.pop(p), s3.pop(p))
    ffn_slice(FF_SLICES - 1)

    top, bot = slice(0, half), slice(half, TM)
    y_a = _dot(hg_ref[...], _weight(wpa_ref[...]))
    mix = (field[COL_GA] * y_a + field[COL_GB] * y_b).astype(_BF16)
    m = _dot(mix, _weight(wout_ref[...]))
    ffn_top = ffn_down(top)
    ffn_bot = ffn_down(bot)
    h1 = _layer_norm(DEEPNORM_ALPHA * x_ref[0] + g1 * m) * ln1w_ref[...] + ln1b_ref[...]
    h1_ref[cur] = h1
    sh2, sc2 = mod_ref[0, 3:4], mod_ref[0, 4:5]
    u2_new = (_layer_norm(h1) * (1.0 + sc2) + sh2).astype(_BF16)
    ffn_finish(top, ffn_top)
    ffn_finish(bot, ffn_bot)
    u2_ref[...] = u2_new


def _block(x, mod, consts):
    n_tiles, _, d = x.shape
    n_batch = mod.shape[0]
    n_seq_tiles = n_tiles // n_batch
    last = n_tiles - 1

    def const_spec(a):
        return pl.BlockSpec(a.shape, lambda t: (0,) * a.ndim, pipeline_mode=pl.Buffered(1))

    return pl.pallas_call(
        functools.partial(_block_kernel, n_seq_tiles),
        grid=(n_tiles + 1,),
        in_specs=[
            pl.BlockSpec((1, TM, d), lambda t: (jnp.minimum(t, last), 0, 0)),
            pl.BlockSpec((1, 6, d), lambda t: (jnp.minimum(t, last) // n_seq_tiles, 0, 0)),
            pl.BlockSpec((1, 6, d), lambda t: (jnp.maximum(t - 1, 0) // n_seq_tiles, 0, 0)),
        ] + [const_spec(a) for a in consts],
        out_specs=pl.BlockSpec((1, TM, d), lambda t: (jnp.maximum(t - 1, 0), 0, 0)),
        out_shape=jax.ShapeDtypeStruct((n_tiles, TM, d), _F32),
        scratch_shapes=[
            pltpu.VMEM((HG_HEADS, HG_DK, HG_DK), _F32),
            pltpu.VMEM((TM, d), _BF16),
            pltpu.VMEM((TM, d), _BF16),
            pltpu.VMEM((2, TM, d), _F32),
            pltpu.VMEM((TM, d), _BF16),
            pltpu.VMEM((TM, D_FF), _BF16),
        ],
        compiler_params=pltpu.CompilerParams(
            dimension_semantics=("arbitrary",),
            vmem_limit_bytes=VMEM_LIMIT),
        name="block",
    )(x, mod, mod, *consts)


def _chunk_tri(n):
    r = jnp.arange(n)[:, None]
    c = jnp.arange(n)[None, :]
    return ((r >= c) & (r // CHUNK == c // CHUNK)).astype(_BF16)


def kernel(x, c, w_ada, b_ada, w_in, b_gate, hgrn_lb_logits, hgrn_norm_w, w_proj_a, gmlp_ln_w,
           gmlp_ln_b, gmlp_ws, gmlp_bs, w_proj_b, w_out, ln1_w, ln1_b, w_ffn_in, w_ffn_out,
           ln2_w, ln2_b):
    assert w_ada.shape[0] == 1, "single-layer block"
    b, s, d = x.shape
    row = lambda a: a.reshape(1, -1)
    mod, lb = _prep(c, w_ada[0], b_ada[0], hgrn_lb_logits)
    mod = mod.transpose(1, 0, 2)

    pos = jnp.arange(GM_BLOCK) // CHUNK
    ws = jnp.where(pos[:, None] >= pos[None, :], gmlp_ws[0], 0.0).astype(_BF16)
    bs = jnp.broadcast_to(gmlp_bs[0][:, :, None], (GM_GROUPS, GM_BLOCK, GM_CG))

    p_in, p_a, p_b, p_out, p_ffn_in, p_ffn_out = _pack_weights(
        w_in[0], w_proj_a[0], w_proj_b[0], w_out[0], w_ffn_in[0], w_ffn_out[0])
    consts = (lb, p_in, _chunk_tri(TM), b_gate[0], row(hgrn_norm_w[0]),
              p_a, row(gmlp_ln_w[0]), row(gmlp_ln_b[0]), ws, bs,
              p_b, p_out, row(ln1_w[0]), row(ln1_b[0]),
              p_ffn_in, p_ffn_out, row(ln2_w[0]), row(ln2_b[0]))
    out = _block(x.reshape(b * s // TM, TM, d), mod, consts)
    return out.reshape(b, s, d)
```

```python
import functools

import jax
import jax.numpy as jnp
from jax import lax
from jax.experimental import pallas as pl
from jax.experimental.pallas import tpu as pltpu

D_MODEL = 1024
CHUNK = 64
SUB_CHUNK = 16
N_SUB = CHUNK // SUB_CHUNK
HG_DK = 128
HG_HEADS = D_MODEL // HG_DK
GM_BLOCK = 128
GM_GROUPS = 8
GM_CG = D_MODEL // GM_GROUPS
D_FF = 2816
DEEPNORM_ALPHA = 2.0 ** 0.25
LN_EPS = 1e-5
RMS_EPS = 1e-6

COL_Q, COL_F, COL_I, COL_G, COL_U, COL_V, COL_GA, COL_GB = (i * D_MODEL for i in range(8))

TM = 256
MXU_N = 256
HEAD_PAIR = MXU_N
N_CHUNKS = TM // CHUNK
N_PAIRS = HG_HEADS // 2
FF_SLICES = D_FF // MXU_N
PACK_STEPS = 8
VMEM_LIMIT = 62 * 1024 * 1024

_F32 = jnp.float32
_BF16 = jnp.bfloat16


def _dot(a, b):
    return jnp.dot(a, b, preferred_element_type=_F32)


def _dot_nt(a, b):
    return lax.dot_general(a, b, (((1,), (1,)), ((), ())), preferred_element_type=_F32)


def _dot_tn(a, b):
    return lax.dot_general(a, b, (((0,), (0,)), ((), ())), preferred_element_type=_F32)


def _pack_kernel(*refs):
    n = len(refs) // 2
    for w_ref, o_ref in zip(refs[:n], refs[n:]):
        o_ref[...] = pltpu.bitcast(w_ref[...].astype(_BF16), jnp.uint32)


def _pack_weights(*ws):
    for w in ws:
        assert w.shape[0] % (16 * PACK_STEPS) == 0, w.shape
    rows = [w.shape[0] // PACK_STEPS for w in ws]
    return pl.pallas_call(
        _pack_kernel,
        grid=(PACK_STEPS,),
        in_specs=[pl.BlockSpec((r, w.shape[1]), lambda i: (i, 0)) for r, w in zip(rows, ws)],
        out_specs=[pl.BlockSpec((r // 2, w.shape[1]), lambda i: (i, 0)) for r, w in zip(rows, ws)],
        out_shape=[jax.ShapeDtypeStruct((w.shape[0] // 2, w.shape[1]), jnp.uint32) for w in ws],
        compiler_params=pltpu.CompilerParams(vmem_limit_bytes=VMEM_LIMIT),
        name="pack",
    )(*ws)


def _weight(packed):
    return pltpu.bitcast(packed, _BF16)


def _layer_norm(x):
    xc = x - jnp.mean(x, axis=-1, keepdims=True)
    return xc * lax.rsqrt(jnp.mean(xc * xc, axis=-1, keepdims=True) + LN_EPS)


def _sigmoid(x):
    return 1.0 / (1.0 + jnp.exp(-x))


def _silu(x):
    return x * _sigmoid(x)


def _gelu(x):
    return 0.5 * x * (1.0 + lax.erf(x * (2.0 ** -0.5)))


def _prep_kernel(c_ref, w_ref, b_ref, lbl_ref, mod_ref, lb_ref):
    cond = _silu(c_ref[...])
    mod_ref[0] = _dot(cond.astype(_BF16), w_ref[...].astype(_BF16)) + b_ref[0]
    logits = lbl_ref[...]
    e = jnp.exp(logits - jnp.max(logits, axis=0, keepdims=True))
    lb_ref[...] = e[0:1] / jnp.sum(e, axis=0, keepdims=True)


def _prep(c, w_ada, b_ada, lb_logits):
    b = c.shape[0]
    return pl.pallas_call(
        _prep_kernel,
        grid=(6,),
        in_specs=[
            pl.BlockSpec((b, D_MODEL), lambda j: (0, 0)),
            pl.BlockSpec((D_MODEL, D_MODEL), lambda j: (0, j)),
            pl.BlockSpec((1, 1, D_MODEL), lambda j: (j, 0, 0)),
            pl.BlockSpec(lb_logits.shape, lambda j: (0, 0)),
        ],
        out_specs=[
            pl.BlockSpec((1, b, D_MODEL), lambda j: (j, 0, 0)),
            pl.BlockSpec((1, D_MODEL), lambda j: (0, 0)),
        ],
        out_shape=[
            jax.ShapeDtypeStruct((6, b, D_MODEL), _F32),
            jax.ShapeDtypeStruct((1, D_MODEL), _F32),
        ],
        name="prep",
    )(c, w_ada, b_ada.reshape(6, 1, D_MODEL), lb_logits)


def _pair_project(u, w_in_ref, p):
    off = p * HEAD_PAIR
    return tuple(_dot(u, _weight(w_in_ref[:, col + off:col + off + HEAD_PAIR]))
                 for col in (COL_Q, COL_F, COL_I, COL_G))


def _pair_gates(zq, zf, lb, tri):
    f = lb + (1.0 - lb) * _sigmoid(zf)
    l2f = jnp.log2(f)
    hi = l2f.astype(_BF16)
    lo = (l2f - hi.astype(_F32)).astype(_BF16)
    cum = _dot(tri, hi) + _dot(tri, lo)
    return _silu(zq), 1.0 - f, cum


def _chunk_head_scores(q, k, cum, v):
    cl = cum[CHUNK - 1:CHUNK]
    v_b = v.astype(_BF16)
    qd = (q * jnp.exp2(cum)).astype(_BF16)
    kd = (k * jnp.exp2(cl - cum)).astype(_BF16)
    upd = _dot_tn(kd, v_b)
    dec = jnp.transpose(jnp.broadcast_to(jnp.exp2(cl), (HG_DK, HG_DK)))

    def keys(ref, hi):
        return (k[0:hi] * jnp.exp2(-cum[0:hi] if ref is None else ref - cum[0:hi])).astype(_BF16)

    def queries(ref, lo):
        return (q[lo:lo + SUB_CHUNK] * jnp.exp2(cum[lo:lo + SUB_CHUNK] - ref)).astype(_BF16)

    zeros = jnp.zeros((SUB_CHUNK, HG_DK), _BF16)
    scores = []
    for i0 in range(0, N_SUB, 2):
        lo0, lo1 = i0 * SUB_CHUNK, (i0 + 1) * SUB_CHUNK
        hi = lo1 + SUB_CHUNK
        ref0 = None if i0 == 0 else cum[lo0 - 1:lo0]
        ref1 = cum[lo1 - 1:lo1]
        q0 = qd[lo0:lo1] if i0 == 0 else queries(ref0, lo0)
        q1 = queries(ref1, lo1)
        lhs = jnp.concatenate([jnp.concatenate([q0, zeros], axis=1),
                               jnp.concatenate([zeros, q1], axis=1)], axis=0)
        rhs = jnp.concatenate([jnp.concatenate([keys(ref0, lo1), zeros], axis=0),
                               keys(ref1, hi)], axis=1)
        scores.append(_dot_nt(lhs, rhs))
    return qd, v_b, dec, upd, scores


def _chunk_head_outputs(st, qd, v_b, scores):
    o_inter = _dot(qd, st.astype(_BF16))
    outs = []
    for j, sc in enumerate(scores):
        lo = 2 * j * SUB_CHUNK
        row = lax.broadcasted_iota(jnp.int32, sc.shape, 0) + lo
        col = lax.broadcasted_iota(jnp.int32, sc.shape, 1)
        sc = jnp.where(col <= row, sc, 0.0).astype(_BF16)
        outs.append(_dot(sc, v_b[0:sc.shape[1]]))
    return o_inter + jnp.concatenate(outs, axis=0)


def _block_kernel(n_seq_tiles,
                  x_ref, mod_ref, modf_ref, lb_ref, w_in_ref, tri_ref, bgate_ref, normw_ref,
                  wpa_ref, lnw_ref, lnb_ref, ws_ref, bs_ref, wpb_ref, wout_ref, ln1w_ref, ln1b_ref,
                  w1_ref, w2_ref, ln2w_ref, ln2b_ref,
                  out_ref, st_ref, hg_ref, gm_ref, h1_ref, u2_ref, act_ref):
    t = pl.program_id(0)

    @pl.when(t == 0)
    def _():
        h1_ref[...] = jnp.zeros_like(h1_ref)
        u2_ref[...] = jnp.zeros_like(u2_ref)

    @pl.when(t % n_seq_tiles == 0)
    def _():
        st_ref[...] = jnp.zeros_like(st_ref)

    cur, prev = t % 2, (t + 1) % 2
    half = TM // 2

    def ffn_slice(i):
        u2 = u2_ref[...]
        a = _dot(u2, _weight(w1_ref[:, i * MXU_N:(i + 1) * MXU_N]))
        b = _dot(u2, _weight(w1_ref[:, D_FF + i * MXU_N:D_FF + (i + 1) * MXU_N]))
        act_ref[:, i * MXU_N:(i + 1) * MXU_N] = (_silu(a) * b).astype(_BF16)

    def ffn_down(rows):
        return _dot(act_ref[rows, :], _weight(w2_ref[...]))

    def ffn_finish(rows, ffn):
        g2 = modf_ref[0, 5:6]
        r = DEEPNORM_ALPHA * h1_ref[prev, rows, :] + g2 * ffn
        out_ref[0, rows, :] = _layer_norm(r) * ln2w_ref[...] + ln2b_ref[...]

    sh1, sc1, g1 = mod_ref[0, 0:1], mod_ref[0, 1:2], mod_ref[0, 2:3]
    ffn_slice(0)
    ffn_slice(1)
    u = (_layer_norm(x_ref[0]) * (1.0 + sc1) + sh1).astype(_BF16)
    tri = tri_ref[...]
    norm_w = normw_ref[...]

    def chunk_head_slices(c, hh):
        return (slice(c * CHUNK, (c + 1) * CHUNK), slice(hh * HG_DK, (hh + 1) * HG_DK))

    def stage2(z, gates):
        q, k, cum = gates
        return [[_chunk_head_scores(q[sl], k[sl], cum[sl], z[2][sl])
                 for sl in (chunk_head_slices(c, hh) for hh in range(2))]
                for c in range(N_CHUNKS)]

    def stage3(p, s2):
        outs = []
        for hh in range(2):
            st = st_ref[2 * p + hh]
            col = []
            for c in range(N_CHUNKS):
                qd, v_b, dec, upd, scores = s2[c][hh]
                col.append(_chunk_head_outputs(st, qd, v_b, scores))
                st = st * dec + upd
            st_ref[2 * p + hh] = st
            outs.append(col)
        return outs

    def stage4(p, z, s3):
        for hh in range(2):
            hd = 2 * p + hh
            for c in range(N_CHUNKS):
                sl = chunk_head_slices(c, hh)
                o = s3[hh][c]
                o = o * lax.rsqrt(jnp.mean(o * o, axis=-1, keepdims=True) + RMS_EPS) * norm_w
                hg_ref[sl[0], hd * HG_DK:(hd + 1) * HG_DK] = (o * _silu(z[3][sl])).astype(_BF16)

    field = {}
    z, s1, s2, s3 = {}, {}, {}, {}
    fillers = (COL_V, COL_U, COL_GA, COL_GB)
    for r in range(N_PAIRS + len(fillers)):
        if r < N_PAIRS:
            z[r] = _pair_project(u, w_in_ref, r)
        else:
            col = fillers[r - N_PAIRS]
            field[col] = _dot(u, _weight(w_in_ref[:, col:col + D_MODEL]))
        ffn_slice(r + 2)
        p = r - 1
        if 0 <= p < N_PAIRS:
            lb = lb_ref[:, p * HEAD_PAIR:(p + 1) * HEAD_PAIR]
            s1[p] = _pair_gates(z[p][0], z[p][1], lb, tri)
        p = r - 2
        if 0 <= p < N_PAIRS:
            s2[p] = stage2(z[p], s1.pop(p))
        p = r - 3
        if 0 <= p < N_PAIRS:
            s3[p] = stage3(p, s2.pop(p))
        p = r - 4
        if 0 <= p < N_PAIRS:
            stage4(p, z.pop(p), s3.pop(p))
    ffn_slice(FF_SLICES - 1)

    gv = (_layer_norm(_gelu(field[COL_V])) * lnw_ref[...] + lnb_ref[...]).astype(_BF16)
    gu = _gelu(field[COL_U])
    n_blocks = TM // GM_BLOCK
    for g in range(GM_GROUPS):
        cols = slice(g * GM_CG, (g + 1) * GM_CG)
        rhs = jnp.concatenate([gv[nb * GM_BLOCK:(nb + 1) * GM_BLOCK, cols]
                               for nb in range(n_blocks)], axis=1)
        sv = _dot(ws_ref[g], rhs)
        for nb in range(n_blocks):
            rows = slice(nb * GM_BLOCK, (nb + 1) * GM_BLOCK)
            sv_nb = sv[:, nb * GM_CG:(nb + 1) * GM_CG] + bs_ref[g]
            gm_ref[rows, cols] = (gu[rows, cols] * sv_nb).astype(_BF16)

    top, bot = slice(0, half), slice(half, TM)
    y_b = _dot(gm_ref[...], _weight(wpb_ref[...]))
    y_a = _dot(hg_ref[...], _weight(wpa_ref[...]))
    mix = (_sigmoid(field[COL_GA] + bgate_ref[0:1]) * y_a
           + _sigmoid(field[COL_GB] + bgate_ref[1:2]) * y_b).astype(_BF16)
    m = _dot(mix, _weight(wout_ref[...]))
    ffn_top = ffn_down(top)
    ffn_bot = ffn_down(bot)
    h1 = _layer_norm(DEEPNORM_ALPHA * x_ref[0] + g1 * m) * ln1w_ref[...] + ln1b_ref[...]
    h1_ref[cur] = h1
    sh2, sc2 = mod_ref[0, 3:4], mod_ref[0, 4:5]
    u2_new = (_layer_norm(h1) * (1.0 + sc2) + sh2).astype(_BF16)
    ffn_finish(top, ffn_top)
    ffn_finish(bot, ffn_bot)
    u2_ref[...] = u2_new


def _block(x, mod, consts):
    n_tiles, _, d = x.shape
    n_batch = mod.shape[0]
    n_seq_tiles = n_tiles // n_batch
    last = n_tiles - 1

    def const_spec(a):
        return pl.BlockSpec(a.shape, lambda t: (0,) * a.ndim, pipeline_mode=pl.Buffered(1))

    return pl.pallas_call(
        functools.partial(_block_kernel, n_seq_tiles),
        grid=(n_tiles + 1,),
        in_specs=[
            pl.BlockSpec((1, TM, d), lambda t: (jnp.minimum(t, last), 0, 0)),
            pl.BlockSpec((1, 6, d), lambda t: (jnp.minimum(t, last) // n_seq_tiles, 0, 0)),
            pl.BlockSpec((1, 6, d), lambda t: (jnp.maximum(t - 1, 0) // n_seq_tiles, 0, 0)),
        ] + [const_spec(a) for a in consts],
        out_specs=pl.BlockSpec((1, TM, d), lambda t: (jnp.maximum(t - 1, 0), 0, 0)),
        out_shape=jax.ShapeDtypeStruct((n_tiles, TM, d), _F32),
        scratch_shapes=[
            pltpu.VMEM((HG_HEADS, HG_DK, HG_DK), _F32),
            pltpu.VMEM((TM, d), _BF16),
            pltpu.VMEM((TM, d), _BF16),
            pltpu.VMEM((2, TM, d), _F32),
            pltpu.VMEM((TM, d), _BF16),
            pltpu.VMEM((TM, D_FF), _BF16),
        ],
        compiler_params=pltpu.CompilerParams(
            dimension_semantics=("arbitrary",),
            vmem_limit_bytes=VMEM_LIMIT),
        name="block",
    )(x, mod, mod, *consts)


def _chunk_tri(n):
    r = jnp.arange(n)[:, None]
    c = jnp.arange(n)[None, :]
    return ((r >= c) & (r // CHUNK == c // CHUNK)).astype(_BF16)


def kernel(x, c, w_ada, b_ada, w_in, b_gate, hgrn_lb_logits, hgrn_norm_w, w_proj_a, gmlp_ln_w,
           gmlp_ln_b, gmlp_ws, gmlp_bs, w_proj_b, w_out, ln1_w, ln1_b, w_ffn_in, w_ffn_out,
           ln2_w, ln2_b):
    assert w_ada.shape[0] == 1, "single-layer block"
    b, s, d = x.shape
    row = lambda a: a.reshape(1, -1)
    mod, lb = _prep(c, w_ada[0], b_ada[0], hgrn_lb_logits)
    mod = mod.transpose(1, 0, 2)

    pos = jnp.arange(GM_BLOCK) // CHUNK
    ws = jnp.where(pos[:, None] >= pos[None, :], gmlp_ws[0], 0.0).astype(_BF16)
    bs = jnp.broadcast_to(gmlp_bs[0][:, :, None], (GM_GROUPS, GM_BLOCK, GM_CG))

    p_in, p_a, p_b, p_out, p_ffn_in, p_ffn_out = _pack_weights(
        w_in[0], w_proj_a[0], w_proj_b[0], w_out[0], w_ffn_in[0], w_ffn_out[0])
    consts = (lb, p_in, _chunk_tri(TM), b_gate[0], row(hgrn_norm_w[0]),
              p_a, row(gmlp_ln_w[0]), row(gmlp_ln_b[0]), ws, bs,
              p_b, p_out, row(ln1_w[0]), row(ln1_b[0]),
              p_ffn_in, p_ffn_out, row(ln2_w[0]), row(ln2_b[0]))
    out = _block(x.reshape(b * s // TM, TM, d), mod, consts)
    return out.reshape(b, s, d)
```

```python
import functools

import jax
import jax.numpy as jnp
from jax import lax
from jax.experimental import pallas as pl
from jax.experimental.pallas import tpu as pltpu

D_MODEL = 1024
CHUNK = 64
SUB_CHUNK = 16
N_SUB = CHUNK // SUB_CHUNK
HG_DK = 128
HG_HEADS = D_MODEL // HG_DK
GM_BLOCK = 128
GM_GROUPS = 8
GM_CG = D_MODEL // GM_GROUPS
D_FF = 2816
DEEPNORM_ALPHA = 2.0 ** 0.25
LN_EPS = 1e-5
RMS_EPS = 1e-6

COL_Q, COL_F, COL_I, COL_G, COL_U, COL_V, COL_GA, COL_GB = (i * D_MODEL for i in range(8))

TM = 256
MXU_N = 256
HEAD_PAIR = MXU_N
N_CHUNKS = TM // CHUNK
N_PAIRS = HG_HEADS // 2
FF_SLICES = D_FF // MXU_N
FFN_LEAD = 3
PACK_STEPS = 8
VMEM_LIMIT = 62 * 1024 * 1024

_F32 = jnp.float32
_BF16 = jnp.bfloat16


def _dot(a, b):
    return jnp.dot(a, b, preferred_element_type=_F32)


def _dot_nt(a, b):
    return lax.dot_general(a, b, (((1,), (1,)), ((), ())), preferred_element_type=_F32)


def _dot_tn(a, b):
    return lax.dot_general(a, b, (((0,), (0,)), ((), ())), preferred_element_type=_F32)


def _pack_kernel(*refs):
    n = len(refs) // 2
    for w_ref, o_ref in zip(refs[:n], refs[n:]):
        o_ref[...] = pltpu.bitcast(w_ref[...].astype(_BF16), jnp.uint32)


def _pack_weights(*ws):
    for w in ws:
        assert w.shape[0] % (16 * PACK_STEPS) == 0, w.shape
    rows = [w.shape[0] // PACK_STEPS for w in ws]
    return pl.pallas_call(
        _pack_kernel,
        grid=(PACK_STEPS,),
        in_specs=[pl.BlockSpec((r, w.shape[1]), lambda i: (i, 0)) for r, w in zip(rows, ws)],
        out_specs=[pl.BlockSpec((r // 2, w.shape[1]), lambda i: (i, 0)) for r, w in zip(rows, ws)],
        out_shape=[jax.ShapeDtypeStruct((w.shape[0] // 2, w.shape[1]), jnp.uint32) for w in ws],
        compiler_params=pltpu.CompilerParams(vmem_limit_bytes=VMEM_LIMIT),
        name="pack",
    )(*ws)


def _weight(packed):
    return pltpu.bitcast(packed, _BF16)


def _layer_norm(x):
    xc = x - jnp.mean(x, axis=-1, keepdims=True)
    return xc * lax.rsqrt(jnp.mean(xc * xc, axis=-1, keepdims=True) + LN_EPS)


def _sigmoid(x):
    return 1.0 / (1.0 + jnp.exp(-x))


def _silu(x):
    return x * _sigmoid(x)


def _gelu(x):
    return 0.5 * x * (1.0 + lax.erf(x * (2.0 ** -0.5)))


def _prep_kernel(c_ref, w_ref, b_ref, lbl_ref, mod_ref, lb_ref):
    cond = _silu(c_ref[...])
    mod_ref[0] = _dot(cond.astype(_BF16), w_ref[...].astype(_BF16)) + b_ref[0]
    logits = lbl_ref[...]
    e = jnp.exp(logits - jnp.max(logits, axis=0, keepdims=True))
    lb_ref[...] = e[0:1] / jnp.sum(e, axis=0, keepdims=True)


def _prep(c, w_ada, b_ada, lb_logits):
    b = c.shape[0]
    return pl.pallas_call(
        _prep_kernel,
        grid=(6,),
        in_specs=[
            pl.BlockSpec((b, D_MODEL), lambda j: (0, 0)),
            pl.BlockSpec((D_MODEL, D_MODEL), lambda j: (0, j)),
            pl.BlockSpec((1, 1, D_MODEL), lambda j: (j, 0, 0)),
            pl.BlockSpec(lb_logits.shape, lambda j: (0, 0)),
        ],
        out_specs=[
            pl.BlockSpec((1, b, D_MODEL), lambda j: (j, 0, 0)),
            pl.BlockSpec((1, D_MODEL), lambda j: (0, 0)),
        ],
        out_shape=[
            jax.ShapeDtypeStruct((6, b, D_MODEL), _F32),
            jax.ShapeDtypeStruct((1, D_MODEL), _F32),
        ],
        name="prep",
    )(c, w_ada, b_ada.reshape(6, 1, D_MODEL), lb_logits)


def _pair_project(u, w_in_ref, p):
    off = p * HEAD_PAIR
    return tuple(_dot(u, _weight(w_in_ref[:, col + off:col + off + HEAD_PAIR]))
                 for col in (COL_Q, COL_F, COL_I, COL_G))


def _pair_gates(zq, zf, lb, tri):
    f = lb + (1.0 - lb) * _sigmoid(zf)
    l2f = jnp.log2(f)
    hi = l2f.astype(_BF16)
    lo = (l2f - hi.astype(_F32)).astype(_BF16)
    cum = _dot(tri, hi) + _dot(tri, lo)
    return _silu(zq), 1.0 - f, cum


def _chunk_head_scores(q, k, cum, v):
    cl = cum[CHUNK - 1:CHUNK]
    v_b = v.astype(_BF16)
    qd = (q * jnp.exp2(cum)).astype(_BF16)
    kd = (k * jnp.exp2(cl - cum)).astype(_BF16)
    upd = _dot_tn(kd, v_b)
    dec = jnp.transpose(jnp.broadcast_to(jnp.exp2(cl), (HG_DK, HG_DK)))

    def keys(ref, hi):
        return (k[0:hi] * jnp.exp2(-cum[0:hi] if ref is None else ref - cum[0:hi])).astype(_BF16)

    def queries(ref, lo):
        return (q[lo:lo + SUB_CHUNK] * jnp.exp2(cum[lo:lo + SUB_CHUNK] - ref)).astype(_BF16)

    zeros = jnp.zeros((SUB_CHUNK, HG_DK), _BF16)
    scores = []
    for i0 in range(0, N_SUB, 2):
        lo0, lo1 = i0 * SUB_CHUNK, (i0 + 1) * SUB_CHUNK
        hi = lo1 + SUB_CHUNK
        ref0 = None if i0 == 0 else cum[lo0 - 1:lo0]
        ref1 = cum[lo1 - 1:lo1]
        q0 = qd[lo0:lo1] if i0 == 0 else queries(ref0, lo0)
        q1 = queries(ref1, lo1)
        lhs = jnp.concatenate([jnp.concatenate([q0, zeros], axis=1),
                               jnp.concatenate([zeros, q1], axis=1)], axis=0)
        rhs = jnp.concatenate([jnp.concatenate([keys(ref0, lo1), zeros], axis=0),
                               keys(ref1, hi)], axis=1)
        scores.append(_dot_nt(lhs, rhs))
    return qd, v_b, dec, upd, scores


def _chunk_head_outputs(st, qd, v_b, scores):
    o_inter = _dot(qd, st.astype(_BF16))
    outs = []
    for j, sc in enumerate(scores):
        lo = 2 * j * SUB_CHUNK
        row = lax.broadcasted_iota(jnp.int32, sc.shape, 0) + lo
        col = lax.broadcasted_iota(jnp.int32, sc.shape, 1)
        sc = jnp.where(col <= row, sc, 0.0).astype(_BF16)
        outs.append(_dot(sc, v_b[0:sc.shape[1]]))
    return o_inter + jnp.concatenate(outs, axis=0)


def _block_kernel(n_seq_tiles,
                  xa_ref, xb_ref, moda_ref, modb_ref, modf_ref, lb_ref, w_in_ref, tri_ref, bgate_ref,
                  normw_ref, wpa_ref, lnw_ref, lnb_ref, ws_ref, bs_ref, wpb_ref, wout_ref, ln1w_ref,
                  ln1b_ref, w1_ref, w2_ref, ln2w_ref, ln2b_ref,
                  out_ref, st_ref, hg_ref, gm_ref, h1_ref, u2_ref, act_ref):
    s = pl.program_id(0)

    @pl.when(s == 0)
    def _():
        st_ref[...] = jnp.zeros_like(st_ref)
        h1_ref[...] = jnp.zeros_like(h1_ref)
        u2_ref[...] = jnp.zeros_like(u2_ref)

    half = TM // 2
    tri = tri_ref[...]
    norm_w = normw_ref[...]

    def chunk_head_slices(c, hh):
        return (slice(c * CHUNK, (c + 1) * CHUNK), slice(hh * HG_DK, (hh + 1) * HG_DK))

    def tile_step(tile, x_ref, mod_ref, modf_ref, slot, out_idx):
        prev = 1 - slot
        keep = jnp.where(tile % n_seq_tiles == 0, 0.0, 1.0).astype(_F32)

        def ffn_slice(i):
            u2 = u2_ref[prev]
            a = _dot(u2, _weight(w1_ref[:, i * MXU_N:(i + 1) * MXU_N]))
            b = _dot(u2, _weight(w1_ref[:, D_FF + i * MXU_N:D_FF + (i + 1) * MXU_N]))
            act_ref[:, i * MXU_N:(i + 1) * MXU_N] = (_silu(a) * b).astype(_BF16)

        def ffn_down(rows):
            return _dot(act_ref[rows, :], _weight(w2_ref[...]))

        def ffn_finish(rows, ffn):
            g2 = modf_ref[0, 5:6]
            r = DEEPNORM_ALPHA * h1_ref[prev, rows, :] + g2 * ffn
            out_ref[out_idx, rows, :] = _layer_norm(r) * ln2w_ref[...] + ln2b_ref[...]

        sh1, sc1, g1 = mod_ref[0, 0:1], mod_ref[0, 1:2], mod_ref[0, 2:3]
        for i in range(FFN_LEAD):
            ffn_slice(i)
        u = (_layer_norm(x_ref[0]) * (1.0 + sc1) + sh1).astype(_BF16)

        def stage2(z, gates):
            q, k, cum = gates
            return [[_chunk_head_scores(q[sl], k[sl], cum[sl], z[2][sl])
                     for sl in (chunk_head_slices(c, hh) for hh in range(2))]
                    for c in range(N_CHUNKS)]

        def stage3(p, s2):
            outs = []
            for hh in range(2):
                st = st_ref[2 * p + hh] * keep
                col = []
                for c in range(N_CHUNKS):
                    qd, v_b, dec, upd, scores = s2[c][hh]
                    col.append(_chunk_head_outputs(st, qd, v_b, scores))
                    st = st * dec + upd
                st_ref[2 * p + hh] = st
                outs.append(col)
            return outs

        def stage4(p, z, s3):
            for hh in range(2):
                hd = 2 * p + hh
                for c in range(N_CHUNKS):
                    sl = chunk_head_slices(c, hh)
                    o = s3[hh][c]
                    o = o * lax.rsqrt(jnp.mean(o * o, axis=-1, keepdims=True) + RMS_EPS) * norm_w
                    hg_ref[sl[0], hd * HG_DK:(hd + 1) * HG_DK] = (
                        o * _silu(z[3][sl])).astype(_BF16)

        field = {}
        z, s1, s2, s3 = {}, {}, {}, {}
        fillers = (COL_V, COL_U, COL_GA, COL_GB)
        next_slice = FFN_LEAD
        for r in range(N_PAIRS + len(fillers)):
            if r < N_PAIRS:
                z[r] = _pair_project(u, w_in_ref, r)
            else:
                col = fillers[r - N_PAIRS]
                field[col] = _dot(u, _weight(w_in_ref[:, col:col + D_MODEL]))
            if next_slice < FF_SLICES:
                ffn_slice(next_slice)
                next_slice += 1
            p = r - 1
            if 0 <= p < N_PAIRS:
                lb = lb_ref[:, p * HEAD_PAIR:(p + 1) * HEAD_PAIR]
                s1[p] = _pair_gates(z[p][0], z[p][1], lb, tri)
            p = r - 2
            if 0 <= p < N_PAIRS:
                s2[p] = stage2(z[p], s1.pop(p))
            p = r - 3
            if 0 <= p < N_PAIRS:
                s3[p] = stage3(p, s2.pop(p))
            p = r - 4
            if 0 <= p < N_PAIRS:
                stage4(p, z.pop(p), s3.pop(p))
        while next_slice < FF_SLICES:
            ffn_slice(next_slice)
            next_slice += 1

        gv = (_layer_norm(_gelu(field[COL_V])) * lnw_ref[...] + lnb_ref[...]).astype(_BF16)
        gu = _gelu(field[COL_U])
        n_blocks = TM // GM_BLOCK
        for g in range(GM_GROUPS):
            cols = slice(g * GM_CG, (g + 1) * GM_CG)
            rhs = jnp.concatenate([gv[nb * GM_BLOCK:(nb + 1) * GM_BLOCK, cols]
                                   for nb in range(n_blocks)], axis=1)
            sv = _dot(ws_ref[g], rhs)
            for nb in range(n_blocks):
                rows = slice(nb * GM_BLOCK, (nb + 1) * GM_BLOCK)
                sv_nb = sv[:, nb * GM_CG:(nb + 1) * GM_CG] + bs_ref[g]
                gm_ref[rows, cols] = (gu[rows, cols] * sv_nb).astype(_BF16)

        top, bot = slice(0, half), slice(half, TM)
        y_b = _dot(gm_ref[...], _weight(wpb_ref[...]))
        y_a = _dot(hg_ref[...], _weight(wpa_ref[...]))
        mix = (_sigmoid(field[COL_GA] + bgate_ref[0:1]) * y_a
               + _sigmoid(field[COL_GB] + bgate_ref[1:2]) * y_b).astype(_BF16)
        m = _dot(mix, _weight(wout_ref[...]))
        ffn_top = ffn_down(top)
        ffn_bot = ffn_down(bot)
        h1 = _layer_norm(DEEPNORM_ALPHA * x_ref[0] + g1 * m) * ln1w_ref[...] + ln1b_ref[...]
        h1_ref[slot] = h1
        sh2, sc2 = mod_ref[0, 3:4], mod_ref[0, 4:5]
        u2_ref[slot] = (_layer_norm(h1) * (1.0 + sc2) + sh2).astype(_BF16)
        ffn_finish(top, ffn_top)
        ffn_finish(bot, ffn_bot)

    tile_step(2 * s - 1, xa_ref, moda_ref, modf_ref, 0, 0)
    tile_step(2 * s, xb_ref, modb_ref, moda_ref, 1, 1)


def _block(x, mod, consts):
    n_tiles, _, d = x.shape
    n_batch = mod.shape[0]
    n_seq_tiles = n_tiles // n_batch
    last = n_tiles - 1
    assert n_tiles % 2 == 0 and n_seq_tiles % 2 == 0

    def const_spec(a):
        return pl.BlockSpec(a.shape, lambda s: (0,) * a.ndim, pipeline_mode=pl.Buffered(1))

    def tile_spec(offset):
        return pl.BlockSpec((1, TM, d), lambda s: (jnp.clip(2 * s + offset, 0, last), 0, 0))

    def mod_spec(offset):
        return pl.BlockSpec(
            (1, 6, d), lambda s: (jnp.clip(2 * s + offset, 0, last) // n_seq_tiles, 0, 0))

    return pl.pallas_call(
        functools.partial(_block_kernel, n_seq_tiles),
        grid=(n_tiles // 2 + 1,),
        in_specs=[tile_spec(-1), tile_spec(0), mod_spec(-1), mod_spec(0), mod_spec(-2)]
        + [const_spec(a) for a in consts],
        out_specs=pl.BlockSpec((2, TM, d), lambda s: (jnp.maximum(s - 1, 0), 0, 0)),
        out_shape=jax.ShapeDtypeStruct((n_tiles, TM, d), _F32),
        scratch_shapes=[
            pltpu.VMEM((HG_HEADS, HG_DK, HG_DK), _F32),
            pltpu.VMEM((TM, d), _BF16),
            pltpu.VMEM((TM, d), _BF16),
            pltpu.VMEM((2, TM, d), _F32),
            pltpu.VMEM((2, TM, d), _BF16),
            pltpu.VMEM((TM, D_FF), _BF16),
        ],
        compiler_params=pltpu.CompilerParams(
            dimension_semantics=("arbitrary",),
            vmem_limit_bytes=VMEM_LIMIT),
        name="block",
    )(x, x, mod, mod, mod, *consts)


def _chunk_tri(n):
    r = jnp.arange(n)[:, None]
    c = jnp.arange(n)[None, :]
    return ((r >= c) & (r // CHUNK == c // CHUNK)).astype(_BF16)


def kernel(x, c, w_ada, b_ada, w_in, b_gate, hgrn_lb_logits, hgrn_norm_w, w_proj_a, gmlp_ln_w,
           gmlp_ln_b, gmlp_ws, gmlp_bs, w_proj_b, w_out, ln1_w, ln1_b, w_ffn_in, w_ffn_out,
           ln2_w, ln2_b):
    assert w_ada.shape[0] == 1, "single-layer block"
    b, s, d = x.shape
    row = lambda a: a.reshape(1, -1)
    mod, lb = _prep(c, w_ada[0], b_ada[0], hgrn_lb_logits)
    mod = mod.transpose(1, 0, 2)

    pos = jnp.arange(GM_BLOCK) // CHUNK
    ws = jnp.where(pos[:, None] >= pos[None, :], gmlp_ws[0], 0.0).astype(_BF16)
    bs = jnp.broadcast_to(gmlp_bs[0][:, :, None], (GM_GROUPS, GM_BLOCK, GM_CG))

    p_in, p_a, p_b, p_out, p_ffn_in, p_ffn_out = _pack_weights(
        w_in[0], w_proj_a[0], w_proj_b[0], w_out[0], w_ffn_in[0], w_ffn_out[0])
    consts = (lb, p_in, _chunk_tri(TM), b_gate[0], row(hgrn_norm_w[0]),
              p_a, row(gmlp_ln_w[0]), row(gmlp_ln_b[0]), ws, bs,
              p_b, p_out, row(ln1_w[0]), row(ln1_b[0]),
              p_ffn_in, p_ffn_out, row(ln2_w[0]), row(ln2_b[0]))
    out = _block(x.reshape(b * s // TM, TM, d), mod, consts)
    return out.reshape(b, s, d)
```

```python
import functools

import jax
import jax.numpy as jnp
from jax import lax
from jax.experimental import pallas as pl
from jax.experimental.pallas import tpu as pltpu

D_MODEL = 1024
CHUNK = 64
SUB_CHUNK = 16
N_SUB = CHUNK // SUB_CHUNK
HG_DK = 128
HG_HEADS = D_MODEL // HG_DK
GM_BLOCK = 128
GM_GROUPS = 8
GM_CG = D_MODEL // GM_GROUPS
D_FF = 2816
DEEPNORM_ALPHA = 2.0 ** 0.25
LN_EPS = 1e-5
RMS_EPS = 1e-6

COL_Q, COL_F, COL_I, COL_G, COL_U, COL_V, COL_GA, COL_GB = (i * D_MODEL for i in range(8))

TM = 256
MXU_N = 256
HEAD_PAIR = MXU_N
N_CHUNKS = TM // CHUNK
N_PAIRS = HG_HEADS // 2
FF_SLICES = D_FF // MXU_N
FFN_LEAD = 3
PACK_STEPS = 8
VMEM_LIMIT = 62 * 1024 * 1024

_F32 = jnp.float32
_BF16 = jnp.bfloat16


def _dot(a, b):
    return jnp.dot(a, b, preferred_element_type=_F32)


def _dot_nt(a, b):
    return lax.dot_general(a, b, (((1,), (1,)), ((), ())), preferred_element_type=_F32)


def _dot_tn(a, b):
    return lax.dot_general(a, b, (((0,), (0,)), ((), ())), preferred_element_type=_F32)


def _pack_kernel(*refs):
    n = len(refs) // 2
    for w_ref, o_ref in zip(refs[:n], refs[n:]):
        o_ref[...] = pltpu.bitcast(w_ref[...].astype(_BF16), jnp.uint32)


def _pack_weights(*ws):
    for w in ws:
        assert w.shape[0] % (16 * PACK_STEPS) == 0, w.shape
    rows = [w.shape[0] // PACK_STEPS for w in ws]
    return pl.pallas_call(
        _pack_kernel,
        grid=(PACK_STEPS,),
        in_specs=[pl.BlockSpec((r, w.shape[1]), lambda i: (i, 0)) for r, w in zip(rows, ws)],
        out_specs=[pl.BlockSpec((r // 2, w.shape[1]), lambda i: (i, 0)) for r, w in zip(rows, ws)],
        out_shape=[jax.ShapeDtypeStruct((w.shape[0] // 2, w.shape[1]), jnp.uint32) for w in ws],
        compiler_params=pltpu.CompilerParams(vmem_limit_bytes=VMEM_LIMIT),
        name="pack",
    )(*ws)


def _weight(packed):
    return pltpu.bitcast(packed, _BF16)


def _layer_norm(x):
    xc = x - jnp.mean(x, axis=-1, keepdims=True)
    return xc * lax.rsqrt(jnp.mean(xc * xc, axis=-1, keepdims=True) + LN_EPS)


def _sigmoid(x):
    return 1.0 / (1.0 + jnp.exp(-x))


def _silu(x):
    return x * _sigmoid(x)


def _gelu(x):
    return 0.5 * x * (1.0 + lax.erf(x * (2.0 ** -0.5)))


def _prep_kernel(c_ref, w_ref, b_ref, lbl_ref, mod_ref, lb_ref):
    cond = _silu(c_ref[...])
    mod_ref[0] = _dot(cond.astype(_BF16), w_ref[...].astype(_BF16)) + b_ref[0]
    logits = lbl_ref[...]
    e = jnp.exp(logits - jnp.max(logits, axis=0, keepdims=True))
    lb_ref[...] = e[0:1] / jnp.sum(e, axis=0, keepdims=True)


def _prep(c, w_ada, b_ada, lb_logits):
    b = c.shape[0]
    return pl.pallas_call(
        _prep_kernel,
        grid=(6,),
        in_specs=[
            pl.BlockSpec((b, D_MODEL), lambda j: (0, 0)),
            pl.BlockSpec((D_MODEL, D_MODEL), lambda j: (0, j)),
            pl.BlockSpec((1, 1, D_MODEL), lambda j: (j, 0, 0)),
            pl.BlockSpec(lb_logits.shape, lambda j: (0, 0)),
        ],
        out_specs=[
            pl.BlockSpec((1, b, D_MODEL), lambda j: (j, 0, 0)),
            pl.BlockSpec((1, D_MODEL), lambda j: (0, 0)),
        ],
        out_shape=[
            jax.ShapeDtypeStruct((6, b, D_MODEL), _F32),
            jax.ShapeDtypeStruct((1, D_MODEL), _F32),
        ],
        name="prep",
    )(c, w_ada, b_ada.reshape(6, 1, D_MODEL), lb_logits)


def _pair_project(u, w_in_ref, p):
    off = p * HEAD_PAIR
    zq, zf, zi, zg = (_dot(u, _weight(w_in_ref[:, col + off:col + off + HEAD_PAIR]))
                      for col in (COL_Q, COL_F, COL_I, COL_G))
    return zq, zf, zi.astype(_BF16), zg


def _pair_gates(zq, zf, lb, tri):
    f = lb + (1.0 - lb) * _sigmoid(zf)
    l2f = jnp.log2(f)
    hi = l2f.astype(_BF16)
    lo = (l2f - hi.astype(_F32)).astype(_BF16)
    cum = _dot(tri, hi) + _dot(tri, lo)
    return _silu(zq), 1.0 - f, cum


def _chunk_head_scores(q, k, cum, v_b):
    cl = cum[CHUNK - 1:CHUNK]
    qd = (q * jnp.exp2(cum)).astype(_BF16)
    kd = (k * jnp.exp2(cl - cum)).astype(_BF16)
    upd = _dot_tn(kd, v_b)
    dec = jnp.transpose(jnp.broadcast_to(jnp.exp2(cl), (HG_DK, HG_DK)))

    def keys(ref, hi):
        return (k[0:hi] * jnp.exp2(-cum[0:hi] if ref is None else ref - cum[0:hi])).astype(_BF16)

    def queries(ref, lo):
        return (q[lo:lo + SUB_CHUNK] * jnp.exp2(cum[lo:lo + SUB_CHUNK] - ref)).astype(_BF16)

    zeros = jnp.zeros((SUB_CHUNK, HG_DK), _BF16)
    scores = []
    for i0 in range(0, N_SUB, 2):
        lo0, lo1 = i0 * SUB_CHUNK, (i0 + 1) * SUB_CHUNK
        hi = lo1 + SUB_CHUNK
        ref0 = None if i0 == 0 else cum[lo0 - 1:lo0]
        ref1 = cum[lo1 - 1:lo1]
        q0 = qd[lo0:lo1] if i0 == 0 else queries(ref0, lo0)
        q1 = queries(ref1, lo1)
        lhs = jnp.concatenate([jnp.concatenate([q0, zeros], axis=1),
                               jnp.concatenate([zeros, q1], axis=1)], axis=0)
        rhs = jnp.concatenate([jnp.concatenate([keys(ref0, lo1), zeros], axis=0),
                               keys(ref1, hi)], axis=1)
        scores.append(_dot_nt(lhs, rhs))
    return qd, v_b, dec, upd, scores


def _chunk_head_outputs(st, qd, v_b, scores):
    o_inter = _dot(qd, st.astype(_BF16))
    outs = []
    for j, sc in enumerate(scores):
        lo = 2 * j * SUB_CHUNK
        row = lax.broadcasted_iota(jnp.int32, sc.shape, 0) + lo
        col = lax.broadcasted_iota(jnp.int32, sc.shape, 1)
        sc = jnp.where(col <= row, sc, 0.0).astype(_BF16)
        outs.append(_dot(sc, v_b[0:sc.shape[1]]))
    return o_inter + jnp.concatenate(outs, axis=0)


def _block_kernel(n_seq_tiles,
                  xa_ref, xb_ref, moda_ref, modb_ref, modf_ref, lb_ref, w_in_ref, tri_ref, bgate_ref,
                  normw_ref, wpa_ref, lnw_ref, lnb_ref, ws_ref, bs_ref, wpb_ref, wout_ref, ln1w_ref,
                  ln1b_ref, w1_ref, w2_ref, ln2w_ref, ln2b_ref,
                  out_ref, st_ref, hg_ref, gm_ref, h1_ref, u2_ref, act_ref):
    s = pl.program_id(0)

    @pl.when(s == 0)
    def _():
        st_ref[...] = jnp.zeros_like(st_ref)
        h1_ref[...] = jnp.zeros_like(h1_ref)
        u2_ref[...] = jnp.zeros_like(u2_ref)

    half = TM // 2
    tri = tri_ref[...]
    norm_w = normw_ref[...]

    def chunk_head_slices(c, hh):
        return (slice(c * CHUNK, (c + 1) * CHUNK), slice(hh * HG_DK, (hh + 1) * HG_DK))

    def tile_step(tile, x_ref, mod_ref, modf_ref, slot, out_idx):
        prev = 1 - slot
        keep = jnp.where(tile % n_seq_tiles == 0, 0.0, 1.0).astype(_F32)

        def ffn_slice(i):
            u2 = u2_ref[prev]
            a = _dot(u2, _weight(w1_ref[:, i * MXU_N:(i + 1) * MXU_N]))
            b = _dot(u2, _weight(w1_ref[:, D_FF + i * MXU_N:D_FF + (i + 1) * MXU_N]))
            act_ref[:, i * MXU_N:(i + 1) * MXU_N] = (_silu(a) * b).astype(_BF16)

        def ffn_down(rows):
            return _dot(act_ref[rows, :], _weight(w2_ref[...]))

        def ffn_finish(rows, ffn):
            g2 = modf_ref[0, 5:6]
            r = DEEPNORM_ALPHA * h1_ref[prev, rows, :] + g2 * ffn
            out_ref[out_idx, rows, :] = _layer_norm(r) * ln2w_ref[...] + ln2b_ref[...]

        sh1, sc1, g1 = mod_ref[0, 0:1], mod_ref[0, 1:2], mod_ref[0, 2:3]
        for i in range(FFN_LEAD):
            ffn_slice(i)
        u = (_layer_norm(x_ref[0]) * (1.0 + sc1) + sh1).astype(_BF16)

        def stage2(z, gates):
            q, k, cum = gates
            return [[_chunk_head_scores(q[sl], k[sl], cum[sl], z[2][sl])
                     for sl in (chunk_head_slices(c, hh) for hh in range(2))]
                    for c in range(N_CHUNKS)]

        def stage3(p, s2):
            outs = []
            for hh in range(2):
                st = st_ref[2 * p + hh] * keep
                col = []
                for c in range(N_CHUNKS):
                    qd, v_b, dec, upd, scores = s2[c][hh]
                    col.append(_chunk_head_outputs(st, qd, v_b, scores))
                    st = st * dec + upd
                st_ref[2 * p + hh] = st
                outs.append(col)
            return outs

        def stage4(p, z, s3):
            for hh in range(2):
                hd = 2 * p + hh
                for c in range(N_CHUNKS):
                    sl = chunk_head_slices(c, hh)
                    o = s3[hh][c]
                    o = o * lax.rsqrt(jnp.mean(o * o, axis=-1, keepdims=True) + RMS_EPS) * norm_w
                    hg_ref[sl[0], hd * HG_DK:(hd + 1) * HG_DK] = (
                        o * _silu(z[3][sl])).astype(_BF16)

        field = {}
        z, s1, s2 = {}, {}, {}
        fillers = (COL_V, COL_U, COL_GA, COL_GB)
        next_slice = FFN_LEAD
        for r in range(N_PAIRS + len(fillers)):
            if r < N_PAIRS:
                z[r] = _pair_project(u, w_in_ref, r)
            else:
                col = fillers[r - N_PAIRS]
                field[col] = _dot(u, _weight(w_in_ref[:, col:col + D_MODEL]))
            if next_slice < FF_SLICES:
                ffn_slice(next_slice)
                next_slice += 1
            p = r - 1
            if 0 <= p < N_PAIRS:
                lb = lb_ref[:, p * HEAD_PAIR:(p + 1) * HEAD_PAIR]
                s1[p] = _pair_gates(z[p][0], z[p][1], lb, tri)
            p = r - 2
            if 0 <= p < N_PAIRS:
                s2[p] = stage2(z[p], s1.pop(p))
            p = r - 3
            if 0 <= p < N_PAIRS:
                stage4(p, z.pop(p), stage3(p, s2.pop(p)))
        while next_slice < FF_SLICES:
            ffn_slice(next_slice)
            next_slice += 1

        gv = (_layer_norm(_gelu(field[COL_V])) * lnw_ref[...] + lnb_ref[...]).astype(_BF16)
        gu = _gelu(field[COL_U])
        n_blocks = TM // GM_BLOCK
        for g in range(GM_GROUPS):
            cols = slice(g * GM_CG, (g + 1) * GM_CG)
            rhs = jnp.concatenate([gv[nb * GM_BLOCK:(nb + 1) * GM_BLOCK, cols]
                                   for nb in range(n_blocks)], axis=1)
            sv = _dot(ws_ref[g], rhs)
            for nb in range(n_blocks):
                rows = slice(nb * GM_BLOCK, (nb + 1) * GM_BLOCK)
                sv_nb = sv[:, nb * GM_CG:(nb + 1) * GM_CG] + bs_ref[g]
                gm_ref[rows, cols] = (gu[rows, cols] * sv_nb).astype(_BF16)

        top, bot = slice(0, half), slice(half, TM)
        y_b = _dot(gm_ref[...], _weight(wpb_ref[...]))
        y_a = _dot(hg_ref[...], _weight(wpa_ref[...]))
        mix = (_sigmoid(field[COL_GA] + bgate_ref[0:1]) * y_a
               + _sigmoid(field[COL_GB] + bgate_ref[1:2]) * y_b).astype(_BF16)
        m = _dot(mix, _weight(wout_ref[...]))
        ffn_top = ffn_down(top)
        ffn_bot = ffn_down(bot)
        h1 = _layer_norm(DEEPNORM_ALPHA * x_ref[0] + g1 * m) * ln1w_ref[...] + ln1b_ref[...]
        h1_ref[slot] = h1
        sh2, sc2 = mod_ref[0, 3:4], mod_ref[0, 4:5]
        u2_ref[slot] = (_layer_norm(h1) * (1.0 + sc2) + sh2).astype(_BF16)
        ffn_finish(top, ffn_top)
        ffn_finish(bot, ffn_bot)

    tile_step(2 * s - 1, xa_ref, moda_ref, modf_ref, 0, 0)
    tile_step(2 * s, xb_ref, modb_ref, moda_ref, 1, 1)


def _block(x, mod, consts):
    n_tiles, _, d = x.shape
    n_batch = mod.shape[0]
    n_seq_tiles = n_tiles // n_batch
    last = n_tiles - 1
    assert n_tiles % 2 == 0 and n_seq_tiles % 2 == 0

    def const_spec(a):
        return pl.BlockSpec(a.shape, lambda s: (0,) * a.ndim, pipeline_mode=pl.Buffered(1))

    def tile_spec(offset):
        return pl.BlockSpec((1, TM, d), lambda s: (jnp.clip(2 * s + offset, 0, last), 0, 0))

    def mod_spec(offset):
        return pl.BlockSpec(
            (1, 6, d), lambda s: (jnp.clip(2 * s + offset, 0, last) // n_seq_tiles, 0, 0))

    return pl.pallas_call(
        functools.partial(_block_kernel, n_seq_tiles),
        grid=(n_tiles // 2 + 1,),
        in_specs=[tile_spec(-1), tile_spec(0), mod_spec(-1), mod_spec(0), mod_spec(-2)]
        + [const_spec(a) for a in consts],
        out_specs=pl.BlockSpec((2, TM, d), lambda s: (jnp.maximum(s - 1, 0), 0, 0)),
        out_shape=jax.ShapeDtypeStruct((n_tiles, TM, d), _F32),
        scratch_shapes=[
            pltpu.VMEM((HG_HEADS, HG_DK, HG_DK), _F32),
            pltpu.VMEM((TM, d), _BF16),
            pltpu.VMEM((TM, d), _BF16),
            pltpu.VMEM((2, TM, d), _F32),
            pltpu.VMEM((2, TM, d), _BF16),
            pltpu.VMEM((TM, D_FF), _BF16),
        ],
        compiler_params=pltpu.CompilerParams(
            dimension_semantics=("arbitrary",),
            vmem_limit_bytes=VMEM_LIMIT),
        name="block",
    )(x, x, mod, mod, mod, *consts)


def _chunk_tri(n):
    r = jnp.arange(n)[:, None]
    c = jnp.arange(n)[None, :]
    return ((r >= c) & (r // CHUNK == c // CHUNK)).astype(_BF16)


def kernel(x, c, w_ada, b_ada, w_in, b_gate, hgrn_lb_logits, hgrn_norm_w, w_proj_a, gmlp_ln_w,
           gmlp_ln_b, gmlp_ws, gmlp_bs, w_proj_b, w_out, ln1_w, ln1_b, w_ffn_in, w_ffn_out,
           ln2_w, ln2_b):
    assert w_ada.shape[0] == 1, "single-layer block"
    b, s, d = x.shape
    row = lambda a: a.reshape(1, -1)
    mod, lb = _prep(c, w_ada[0], b_ada[0], hgrn_lb_logits)
    mod = mod.transpose(1, 0, 2)

    pos = jnp.arange(GM_BLOCK) // CHUNK
    ws = jnp.where(pos[:, None] >= pos[None, :], gmlp_ws[0], 0.0).astype(_BF16)
    bs = jnp.broadcast_to(gmlp_bs[0][:, :, None], (GM_GROUPS, GM_BLOCK, GM_CG))

    p_in, p_a, p_b, p_out, p_ffn_in, p_ffn_out = _pack_weights(
        w_in[0], w_proj_a[0], w_proj_b[0], w_out[0], w_ffn_in[0], w_ffn_out[0])
    consts = (lb, p_in, _chunk_tri(TM), b_gate[0], row(hgrn_norm_w[0]),
              p_a, row(gmlp_ln_w[0]), row(gmlp_ln_b[0]), ws, bs,
              p_b, p_out, row(ln1_w[0]), row(ln1_b[0]),
              p_ffn_in, p_ffn_out, row(ln2_w[0]), row(ln2_b[0]))
    out = _block(x.reshape(b * s // TM, TM, d), mod, consts)
    return out.reshape(b, s, d)
```

```python
import functools

import jax
import jax.numpy as jnp
from jax import lax
from jax.experimental import pallas as pl
from jax.experimental.pallas import tpu as pltpu

D_MODEL = 1024
CHUNK = 64
SUB_CHUNK = 16
N_SUB = CHUNK // SUB_CHUNK
HG_DK = 128
HG_HEADS = D_MODEL // HG_DK
GM_BLOCK = 128
GM_GROUPS = 8
GM_CG = D_MODEL // GM_GROUPS
D_FF = 2816
DEEPNORM_ALPHA = 2.0 ** 0.25
LN_EPS = 1e-5
RMS_EPS = 1e-6

COL_Q, COL_F, COL_I, COL_G, COL_U, COL_V, COL_GA, COL_GB = (i * D_MODEL for i in range(8))

TM = 256
MXU_N = 256
HEAD_PAIR = MXU_N
N_CHUNKS = TM // CHUNK
N_PAIRS = HG_HEADS // 2
FF_SLICES = D_FF // MXU_N
FFN_LEAD = 4
PACK_STEPS = 8
VMEM_LIMIT = 62 * 1024 * 1024

_F32 = jnp.float32
_BF16 = jnp.bfloat16


def _dot(a, b):
    return jnp.dot(a, b, preferred_element_type=_F32)


def _dot_nt(a, b):
    return lax.dot_general(a, b, (((1,), (1,)), ((), ())), preferred_element_type=_F32)


def _dot_tn(a, b):
    return lax.dot_general(a, b, (((0,), (0,)), ((), ())), preferred_element_type=_F32)


def _pack_kernel(*refs):
    n = len(refs) // 2
    for w_ref, o_ref in zip(refs[:n], refs[n:]):
        o_ref[...] = pltpu.bitcast(w_ref[...].astype(_BF16), jnp.uint32)


def _pack_weights(*ws):
    for w in ws:
        assert w.shape[0] % (16 * PACK_STEPS) == 0, w.shape
    rows = [w.shape[0] // PACK_STEPS for w in ws]
    return pl.pallas_call(
        _pack_kernel,
        grid=(PACK_STEPS,),
        in_specs=[pl.BlockSpec((r, w.shape[1]), lambda i: (i, 0)) for r, w in zip(rows, ws)],
        out_specs=[pl.BlockSpec((r // 2, w.shape[1]), lambda i: (i, 0)) for r, w in zip(rows, ws)],
        out_shape=[jax.ShapeDtypeStruct((w.shape[0] // 2, w.shape[1]), jnp.uint32) for w in ws],
        compiler_params=pltpu.CompilerParams(vmem_limit_bytes=VMEM_LIMIT),
        name="pack",
    )(*ws)


def _weight(packed):
    return pltpu.bitcast(packed, _BF16)


def _layer_norm(x):
    xc = x - jnp.mean(x, axis=-1, keepdims=True)
    return xc * lax.rsqrt(jnp.mean(xc * xc, axis=-1, keepdims=True) + LN_EPS)


def _sigmoid(x):
    return 1.0 / (1.0 + jnp.exp(-x))


def _silu(x):
    return x * _sigmoid(x)


def _gelu(x):
    return 0.5 * x * (1.0 + lax.erf(x * (2.0 ** -0.5)))


def _prep_kernel(c_ref, w_ref, b_ref, lbl_ref, mod_ref, lb_ref):
    cond = _silu(c_ref[...])
    mod_ref[0] = _dot(cond.astype(_BF16), w_ref[...].astype(_BF16)) + b_ref[0]
    logits = lbl_ref[...]
    e = jnp.exp(logits - jnp.max(logits, axis=0, keepdims=True))
    lb_ref[...] = e[0:1] / jnp.sum(e, axis=0, keepdims=True)


def _prep(c, w_ada, b_ada, lb_logits):
    b = c.shape[0]
    return pl.pallas_call(
        _prep_kernel,
        grid=(6,),
        in_specs=[
            pl.BlockSpec((b, D_MODEL), lambda j: (0, 0)),
            pl.BlockSpec((D_MODEL, D_MODEL), lambda j: (0, j)),
            pl.BlockSpec((1, 1, D_MODEL), lambda j: (j, 0, 0)),
            pl.BlockSpec(lb_logits.shape, lambda j: (0, 0)),
        ],
        out_specs=[
            pl.BlockSpec((1, b, D_MODEL), lambda j: (j, 0, 0)),
            pl.BlockSpec((1, D_MODEL), lambda j: (0, 0)),
        ],
        out_shape=[
            jax.ShapeDtypeStruct((6, b, D_MODEL), _F32),
            jax.ShapeDtypeStruct((1, D_MODEL), _F32),
        ],
        name="prep",
    )(c, w_ada, b_ada.reshape(6, 1, D_MODEL), lb_logits)


def _pair_project(u, w_in_ref, p):
    off = p * HEAD_PAIR
    zq, zf, zi, zg = (_dot(u, _weight(w_in_ref[:, col + off:col + off + HEAD_PAIR]))
                      for col in (COL_Q, COL_F, COL_I, COL_G))
    return zq, zf, zi.astype(_BF16), zg


def _pair_gates(zq, zf, lb, tri):
    f = lb + (1.0 - lb) * _sigmoid(zf)
    l2f = jnp.log2(f)
    hi = l2f.astype(_BF16)
    lo = (l2f - hi.astype(_F32)).astype(_BF16)
    cum = _dot(tri, hi) + _dot(tri, lo)
    return _silu(zq), 1.0 - f, cum


def _chunk_head_scores(q, k, cum, v_b):
    cl = cum[CHUNK - 1:CHUNK]
    qd = (q * jnp.exp2(cum)).astype(_BF16)
    kd = (k * jnp.exp2(cl - cum)).astype(_BF16)
    upd = _dot_tn(kd, v_b)
    dec = jnp.transpose(jnp.broadcast_to(jnp.exp2(cl), (HG_DK, HG_DK)))

    def keys(ref, hi):
        return (k[0:hi] * jnp.exp2(-cum[0:hi] if ref is None else ref - cum[0:hi])).astype(_BF16)

    def queries(ref, lo):
        return (q[lo:lo + SUB_CHUNK] * jnp.exp2(cum[lo:lo + SUB_CHUNK] - ref)).astype(_BF16)

    zeros = jnp.zeros((SUB_CHUNK, HG_DK), _BF16)
    scores = []
    for i0 in range(0, N_SUB, 2):
        lo0, lo1 = i0 * SUB_CHUNK, (i0 + 1) * SUB_CHUNK
        hi = lo1 + SUB_CHUNK
        ref0 = None if i0 == 0 else cum[lo0 - 1:lo0]
        ref1 = cum[lo1 - 1:lo1]
        q0 = qd[lo0:lo1] if i0 == 0 else queries(ref0, lo0)
        q1 = queries(ref1, lo1)
        lhs = jnp.concatenate([jnp.concatenate([q0, zeros], axis=1),
                               jnp.concatenate([zeros, q1], axis=1)], axis=0)
        rhs = jnp.concatenate([jnp.concatenate([keys(ref0, lo1), zeros], axis=0),
                               keys(ref1, hi)], axis=1)
        scores.append(_dot_nt(lhs, rhs))
    return qd, v_b, dec, upd, scores


def _chunk_head_outputs(st, qd, v_b, scores):
    o_inter = _dot(qd, st.astype(_BF16))
    outs = []
    for j, sc in enumerate(scores):
        lo = 2 * j * SUB_CHUNK
        row = lax.broadcasted_iota(jnp.int32, sc.shape, 0) + lo
        col = lax.broadcasted_iota(jnp.int32, sc.shape, 1)
        sc = jnp.where(col <= row, sc, 0.0).astype(_BF16)
        outs.append(_dot(sc, v_b[0:sc.shape[1]]))
    return o_inter + jnp.concatenate(outs, axis=0)


def _block_kernel(n_seq_tiles,
                  xa_ref, xb_ref, moda_ref, modb_ref, modf_ref, lb_ref, w_in_ref, tri_ref, bgate_ref,
                  normw_ref, wpa_ref, lnw_ref, lnb_ref, ws_ref, bs_ref, wpb_ref, wout_ref, ln1w_ref,
                  ln1b_ref, w1_ref, w2_ref, ln2w_ref, ln2b_ref,
                  out_ref, st_ref, hg_ref, gm_ref, h1_ref, u2_ref, act_ref):
    s = pl.program_id(0)

    @pl.when(s == 0)
    def _():
        st_ref[...] = jnp.zeros_like(st_ref)
        h1_ref[...] = jnp.zeros_like(h1_ref)
        u2_ref[...] = jnp.zeros_like(u2_ref)

    half = TM // 2
    tri = tri_ref[...]
    norm_w = normw_ref[...]

    def chunk_head_slices(c, hh):
        return (slice(c * CHUNK, (c + 1) * CHUNK), slice(hh * HG_DK, (hh + 1) * HG_DK))

    def tile_step(tile, x_ref, mod_ref, modf_ref, slot, out_idx):
        prev = 1 - slot
        keep = jnp.where(tile % n_seq_tiles == 0, 0.0, 1.0).astype(_F32)

        def ffn_slice(i):
            u2 = u2_ref[prev]
            a = _dot(u2, _weight(w1_ref[:, i * MXU_N:(i + 1) * MXU_N]))
            b = _dot(u2, _weight(w1_ref[:, D_FF + i * MXU_N:D_FF + (i + 1) * MXU_N]))
            act_ref[:, i * MXU_N:(i + 1) * MXU_N] = (_silu(a) * b).astype(_BF16)

        def ffn_down(rows):
            return _dot(act_ref[rows, :], _weight(w2_ref[...]))

        def ffn_finish(rows, ffn):
            g2 = modf_ref[0, 5:6]
            r = DEEPNORM_ALPHA * h1_ref[prev, rows, :] + g2 * ffn
            out_ref[out_idx, rows, :] = _layer_norm(r) * ln2w_ref[...] + ln2b_ref[...]

        sh1, sc1, g1 = mod_ref[0, 0:1], mod_ref[0, 1:2], mod_ref[0, 2:3]
        for i in range(FFN_LEAD):
            ffn_slice(i)
        u = (_layer_norm(x_ref[0]) * (1.0 + sc1) + sh1).astype(_BF16)

        def stage2(z, gates):
            q, k, cum = gates
            return [[_chunk_head_scores(q[sl], k[sl], cum[sl], z[2][sl])
                     for sl in (chunk_head_slices(c, hh) for hh in range(2))]
                    for c in range(N_CHUNKS)]

        def stage3(p, s2):
            outs = []
            for hh in range(2):
                st = st_ref[2 * p + hh] * keep
                col = []
                for c in range(N_CHUNKS):
                    qd, v_b, dec, upd, scores = s2[c][hh]
                    col.append(_chunk_head_outputs(st, qd, v_b, scores))
                    st = st * dec + upd
                st_ref[2 * p + hh] = st
                outs.append(col)
            return outs

        def stage4(p, z, s3):
            for hh in range(2):
                hd = 2 * p + hh
                for c in range(N_CHUNKS):
                    sl = chunk_head_slices(c, hh)
                    o = s3[hh][c]
                    o = o * lax.rsqrt(jnp.mean(o * o, axis=-1, keepdims=True) + RMS_EPS) * norm_w
                    hg_ref[sl[0], hd * HG_DK:(hd + 1) * HG_DK] = (
                        o * _silu(z[3][sl])).astype(_BF16)

        field = {}
        z, s1, s2 = {}, {}, {}
        fillers = (COL_V, COL_U, COL_GA, COL_GB)
        next_slice = FFN_LEAD
        for r in range(N_PAIRS + len(fillers)):
            if r < N_PAIRS:
                z[r] = _pair_project(u, w_in_ref, r)
            else:
                col = fillers[r - N_PAIRS]
                field[col] = _dot(u, _weight(w_in_ref[:, col:col + D_MODEL]))
            if next_slice < FF_SLICES:
                ffn_slice(next_slice)
                next_slice += 1
            p = r - 1
            if 0 <= p < N_PAIRS:
                lb = lb_ref[:, p * HEAD_PAIR:(p + 1) * HEAD_PAIR]
                s1[p] = _pair_gates(z[p][0], z[p][1], lb, tri)
            p = r - 2
            if 0 <= p < N_PAIRS:
                s2[p] = stage2(z[p], s1.pop(p))
            p = r - 3
            if 0 <= p < N_PAIRS:
                stage4(p, z.pop(p), stage3(p, s2.pop(p)))
        while next_slice < FF_SLICES:
            ffn_slice(next_slice)
            next_slice += 1

        gv = (_layer_norm(_gelu(field[COL_V])) * lnw_ref[...] + lnb_ref[...]).astype(_BF16)
        gu = _gelu(field[COL_U])
        n_blocks = TM // GM_BLOCK
        for g in range(GM_GROUPS):
            cols = slice(g * GM_CG, (g + 1) * GM_CG)
            rhs = jnp.concatenate([gv[nb * GM_BLOCK:(nb + 1) * GM_BLOCK, cols]
                                   for nb in range(n_blocks)], axis=1)
            sv = _dot(ws_ref[g], rhs)
            for nb in range(n_blocks):
                rows = slice(nb * GM_BLOCK, (nb + 1) * GM_BLOCK)
                sv_nb = sv[:, nb * GM_CG:(nb + 1) * GM_CG] + bs_ref[g]
                gm_ref[rows, cols] = (gu[rows, cols] * sv_nb).astype(_BF16)

        top, bot = slice(0, half), slice(half, TM)
        y_b = _dot(gm_ref[...], _weight(wpb_ref[...]))
        y_a = _dot(hg_ref[...], _weight(wpa_ref[...]))
        mix = (_sigmoid(field[COL_GA] + bgate_ref[0:1]) * y_a
               + _sigmoid(field[COL_GB] + bgate_ref[1:2]) * y_b).astype(_BF16)
        m = _dot(mix, _weight(wout_ref[...]))
        ffn_top = ffn_down(top)
        ffn_bot = ffn_down(bot)
        h1 = _layer_norm(DEEPNORM_ALPHA * x_ref[0] + g1 * m) * ln1w_ref[...] + ln1b_ref[...]
        h1_ref[slot] = h1
        sh2, sc2 = mod_ref[0, 3:4], mod_ref[0, 4:5]
        u2_ref[slot] = (_layer_norm(h1) * (1.0 + sc2) + sh2).astype(_BF16)
        ffn_finish(top, ffn_top)
        ffn_finish(bot, ffn_bot)

    tile_step(2 * s - 1, xa_ref, moda_ref, modf_ref, 0, 0)
    tile_step(2 * s, xb_ref, modb_ref, moda_ref, 1, 1)


def _block(x, mod, consts):
    n_tiles, _, d = x.shape
    n_batch = mod.shape[0]
    n_seq_tiles = n_tiles // n_batch
    last = n_tiles - 1
    assert n_tiles % 2 == 0 and n_seq_tiles % 2 == 0

    def const_spec(a):
        return pl.BlockSpec(a.shape, lambda s: (0,) * a.ndim, pipeline_mode=pl.Buffered(1))

    def tile_spec(offset):
        return pl.BlockSpec((1, TM, d), lambda s: (jnp.clip(2 * s + offset, 0, last), 0, 0))

    def mod_spec(offset):
        return pl.BlockSpec(
            (1, 6, d), lambda s: (jnp.clip(2 * s + offset, 0, last) // n_seq_tiles, 0, 0))

    return pl.pallas_call(
        functools.partial(_block_kernel, n_seq_tiles),
        grid=(n_tiles // 2 + 1,),
        in_specs=[tile_spec(-1), tile_spec(0), mod_spec(-1), mod_spec(0), mod_spec(-2)]
        + [const_spec(a) for a in consts],
        out_specs=pl.BlockSpec((2, TM, d), lambda s: (jnp.maximum(s - 1, 0), 0, 0)),
        out_shape=jax.ShapeDtypeStruct((n_tiles, TM, d), _F32),
        scratch_shapes=[
            pltpu.VMEM((HG_HEADS, HG_DK, HG_DK), _F32),
            pltpu.VMEM((TM, d), _BF16),
            pltpu.VMEM((TM, d), _BF16),
            pltpu.VMEM((2, TM, d), _F32),
            pltpu.VMEM((2, TM, d), _BF16),
            pltpu.VMEM((TM, D_FF), _BF16),
        ],
        compiler_params=pltpu.CompilerParams(
            dimension_semantics=("arbitrary",),
            vmem_limit_bytes=VMEM_LIMIT),
        name="block",
    )(x, x, mod, mod, mod, *consts)


def _chunk_tri(n):
    r = jnp.arange(n)[:, None]
    c = jnp.arange(n)[None, :]
    return ((r >= c) & (r // CHUNK == c // CHUNK)).astype(_BF16)


def kernel(x, c, w_ada, b_ada, w_in, b_gate, hgrn_lb_logits, hgrn_norm_w, w_proj_a, gmlp_ln_w,
           gmlp_ln_b, gmlp_ws, gmlp_bs, w_proj_b, w_out, ln1_w, ln1_b, w_ffn_in, w_ffn_out,
           ln2_w, ln2_b):
    assert w_ada.shape[0] == 1, "single-layer block"
    b, s, d = x.shape
    row = lambda a: a.reshape(1, -1)
    mod, lb = _prep(c, w_ada[0], b_ada[0], hgrn_lb_logits)
    mod = mod.transpose(1, 0, 2)

    pos = jnp.arange(GM_BLOCK) // CHUNK
    ws = jnp.where(pos[:, None] >= pos[None, :], gmlp_ws[0], 0.0).astype(_BF16)
    bs = jnp.broadcast_to(gmlp_bs[0][:, :, None], (GM_GROUPS, GM_BLOCK, GM_CG))

    p_in, p_a, p_b, p_out, p_ffn_in, p_ffn_out = _pack_weights(
        w_in[0], w_proj_a[0], w_proj_b[0], w_out[0], w_ffn_in[0], w_ffn_out[0])
    consts = (lb, p_in, _chunk_tri(TM), b_gate[0], row(hgrn_norm_w[0]),
              p_a, row(gmlp_ln_w[0]), row(gmlp_ln_b[0]), ws, bs,
              p_b, p_out, row(ln1_w[0]), row(ln1_b[0]),
              p_ffn_in, p_ffn_out, row(ln2_w[0]), row(ln2_b[0]))
    out = _block(x.reshape(b * s // TM, TM, d), mod, consts)
    return out.reshape(b, s, d)
```

```python
import functools

import jax
import jax.numpy as jnp
from jax import lax
from jax.experimental import pallas as pl
from jax.experimental.pallas import tpu as pltpu

D_MODEL = 1024
CHUNK = 64
SUB_CHUNK = 16
N_SUB = CHUNK // SUB_CHUNK
HG_DK = 128
HG_HEADS = D_MODEL // HG_DK
GM_BLOCK = 128
GM_GROUPS = 8
GM_CG = D_MODEL // GM_GROUPS
D_FF = 2816
DEEPNORM_ALPHA = 2.0 ** 0.25
LN_EPS = 1e-5
RMS_EPS = 1e-6

COL_Q, COL_F, COL_I, COL_G, COL_U, COL_V, COL_GA, COL_GB = (i * D_MODEL for i in range(8))

TM = 256
MXU_N = 256
HEAD_PAIR = MXU_N
N_CHUNKS = TM // CHUNK
N_PAIRS = HG_HEADS // 2
FF_SLICES = D_FF // MXU_N
FFN_LEAD = 5
PACK_STEPS = 8
VMEM_LIMIT = 63 * 1024 * 1024 + 512 * 1024

_F32 = jnp.float32
_BF16 = jnp.bfloat16


def _dot(a, b):
    return jnp.dot(a, b, preferred_element_type=_F32)


def _dot_nt(a, b):
    return lax.dot_general(a, b, (((1,), (1,)), ((), ())), preferred_element_type=_F32)


def _dot_tn(a, b):
    return lax.dot_general(a, b, (((0,), (0,)), ((), ())), preferred_element_type=_F32)


def _pack_kernel(*refs):
    n = len(refs) // 2
    for w_ref, o_ref in zip(refs[:n], refs[n:]):
        o_ref[...] = pltpu.bitcast(w_ref[...].astype(_BF16), jnp.uint32)


def _pack_weights(*ws):
    for w in ws:
        assert w.shape[0] % (16 * PACK_STEPS) == 0, w.shape
    rows = [w.shape[0] // PACK_STEPS for w in ws]
    return pl.pallas_call(
        _pack_kernel,
        grid=(PACK_STEPS,),
        in_specs=[pl.BlockSpec((r, w.shape[1]), lambda i: (i, 0)) for r, w in zip(rows, ws)],
        out_specs=[pl.BlockSpec((r // 2, w.shape[1]), lambda i: (i, 0)) for r, w in zip(rows, ws)],
        out_shape=[jax.ShapeDtypeStruct((w.shape[0] // 2, w.shape[1]), jnp.uint32) for w in ws],
        compiler_params=pltpu.CompilerParams(vmem_limit_bytes=VMEM_LIMIT),
        name="pack",
    )(*ws)


def _weight(packed):
    return pltpu.bitcast(packed, _BF16)


def _layer_norm(x):
    xc = x - jnp.mean(x, axis=-1, keepdims=True)
    return xc * lax.rsqrt(jnp.mean(xc * xc, axis=-1, keepdims=True) + LN_EPS)


def _sigmoid(x):
    return 1.0 / (1.0 + jnp.exp(-x))


def _silu(x):
    return x * _sigmoid(x)


def _gelu(x):
    return 0.5 * x * (1.0 + lax.erf(x * (2.0 ** -0.5)))


def _prep_kernel(c_ref, w_ref, b_ref, lbl_ref, mod_ref, lb_ref):
    cond = _silu(c_ref[...])
    mod_ref[0] = _dot(cond.astype(_BF16), w_ref[...].astype(_BF16)) + b_ref[0]
    logits = lbl_ref[...]
    e = jnp.exp(logits - jnp.max(logits, axis=0, keepdims=True))
    lb_ref[...] = e[0:1] / jnp.sum(e, axis=0, keepdims=True)


def _prep(c, w_ada, b_ada, lb_logits):
    b = c.shape[0]
    return pl.pallas_call(
        _prep_kernel,
        grid=(6,),
        in_specs=[
            pl.BlockSpec((b, D_MODEL), lambda j: (0, 0)),
            pl.BlockSpec((D_MODEL, D_MODEL), lambda j: (0, j)),
            pl.BlockSpec((1, 1, D_MODEL), lambda j: (j, 0, 0)),
            pl.BlockSpec(lb_logits.shape, lambda j: (0, 0)),
        ],
        out_specs=[
            pl.BlockSpec((1, b, D_MODEL), lambda j: (j, 0, 0)),
            pl.BlockSpec((1, D_MODEL), lambda j: (0, 0)),
        ],
        out_shape=[
            jax.ShapeDtypeStruct((6, b, D_MODEL), _F32),
            jax.ShapeDtypeStruct((1, D_MODEL), _F32),
        ],
        name="prep",
    )(c, w_ada, b_ada.reshape(6, 1, D_MODEL), lb_logits)


def _pair_project(u, w_in_ref, p):
    off = p * HEAD_PAIR
    zq, zf, zi, zg = (_dot(u, _weight(w_in_ref[:, col + off:col + off + HEAD_PAIR]))
                      for col in (COL_Q, COL_F, COL_I, COL_G))
    return zq, zf, zi.astype(_BF16), zg


def _pair_gates(zq, zf, lb, tri):
    f = lb + (1.0 - lb) * _sigmoid(zf)
    l2f = jnp.log2(f)
    hi = l2f.astype(_BF16)
    lo = (l2f - hi.astype(_F32)).astype(_BF16)
    cum = _dot(tri, hi) + _dot(tri, lo)
    return _silu(zq), 1.0 - f, cum


def _chunk_head_scores(q, k, cum, v_b):
    cl = cum[CHUNK - 1:CHUNK]
    qd = (q * jnp.exp2(cum)).astype(_BF16)
    kd = (k * jnp.exp2(cl - cum)).astype(_BF16)
    upd = _dot_tn(kd, v_b)
    dec = jnp.transpose(jnp.broadcast_to(jnp.exp2(cl), (HG_DK, HG_DK)))

    def keys(ref, hi):
        return (k[0:hi] * jnp.exp2(-cum[0:hi] if ref is None else ref - cum[0:hi])).astype(_BF16)

    def queries(ref, lo):
        return (q[lo:lo + SUB_CHUNK] * jnp.exp2(cum[lo:lo + SUB_CHUNK] - ref)).astype(_BF16)

    zeros = jnp.zeros((SUB_CHUNK, HG_DK), _BF16)
    scores = []
    for i0 in range(0, N_SUB, 2):
        lo0, lo1 = i0 * SUB_CHUNK, (i0 + 1) * SUB_CHUNK
        hi = lo1 + SUB_CHUNK
        ref0 = None if i0 == 0 else cum[lo0 - 1:lo0]
        ref1 = cum[lo1 - 1:lo1]
        q0 = qd[lo0:lo1] if i0 == 0 else queries(ref0, lo0)
        q1 = queries(ref1, lo1)
        lhs = jnp.concatenate([jnp.concatenate([q0, zeros], axis=1),
                               jnp.concatenate([zeros, q1], axis=1)], axis=0)
        rhs = jnp.concatenate([jnp.concatenate([keys(ref0, lo1), zeros], axis=0),
                               keys(ref1, hi)], axis=1)
        scores.append(_dot_nt(lhs, rhs))
    return qd, v_b, dec, upd, scores


def _chunk_head_outputs(st, qd, v_b, scores):
    o_inter = _dot(qd, st.astype(_BF16))
    outs = []
    for j, sc in enumerate(scores):
        lo = 2 * j * SUB_CHUNK
        row = lax.broadcasted_iota(jnp.int32, sc.shape, 0) + lo
        col = lax.broadcasted_iota(jnp.int32, sc.shape, 1)
        sc = jnp.where(col <= row, sc, 0.0).astype(_BF16)
        outs.append(_dot(sc, v_b[0:sc.shape[1]]))
    return o_inter + jnp.concatenate(outs, axis=0)


def _block_kernel(n_seq_tiles,
                  xa_ref, xb_ref, moda_ref, modb_ref, modf_ref, lb_ref, w_in_ref, tri_ref, bgate_ref,
                  normw_ref, wpa_ref, lnw_ref, lnb_ref, ws_ref, bs_ref, wpb_ref, wout_ref, ln1w_ref,
                  ln1b_ref, w1_ref, w2_ref, ln2w_ref, ln2b_ref,
                  out_ref, st_ref, hg_ref, gm_ref, h1_ref, u2_ref, act_ref):
    s = pl.program_id(0)

    @pl.when(s == 0)
    def _():
        st_ref[...] = jnp.zeros_like(st_ref)
        h1_ref[...] = jnp.zeros_like(h1_ref)
        u2_ref[...] = jnp.zeros_like(u2_ref)

    half = TM // 2
    tri = tri_ref[...]
    norm_w = normw_ref[...]

    def chunk_head_slices(c, hh):
        return (slice(c * CHUNK, (c + 1) * CHUNK), slice(hh * HG_DK, (hh + 1) * HG_DK))

    def tile_step(tile, x_ref, mod_ref, modf_ref, slot, out_idx):
        prev = 1 - slot
        keep = jnp.where(tile % n_seq_tiles == 0, 0.0, 1.0).astype(_F32)

        def ffn_slice(i):
            u2 = u2_ref[prev]
            a = _dot(u2, _weight(w1_ref[:, i * MXU_N:(i + 1) * MXU_N]))
            b = _dot(u2, _weight(w1_ref[:, D_FF + i * MXU_N:D_FF + (i + 1) * MXU_N]))
            act_ref[:, i * MXU_N:(i + 1) * MXU_N] = (_silu(a) * b).astype(_BF16)

        def ffn_down(rows):
            return _dot(act_ref[rows, :], _weight(w2_ref[...]))

        def ffn_finish(rows, ffn):
            g2 = modf_ref[0, 5:6]
            r = DEEPNORM_ALPHA * h1_ref[prev, rows, :] + g2 * ffn
            out_ref[out_idx, rows, :] = _layer_norm(r) * ln2w_ref[...] + ln2b_ref[...]

        sh1, sc1, g1 = mod_ref[0, 0:1], mod_ref[0, 1:2], mod_ref[0, 2:3]
        for i in range(FFN_LEAD):
            ffn_slice(i)
        u = (_layer_norm(x_ref[0]) * (1.0 + sc1) + sh1).astype(_BF16)

        def stage2(z, gates):
            q, k, cum = gates
            return [[_chunk_head_scores(q[sl], k[sl], cum[sl], z[2][sl])
                     for sl in (chunk_head_slices(c, hh) for hh in range(2))]
                    for c in range(N_CHUNKS)]

        def stage3(p, s2):
            outs = []
            for hh in range(2):
                st = st_ref[2 * p + hh] * keep
                col = []
                for c in range(N_CHUNKS):
                    qd, v_b, dec, upd, scores = s2[c][hh]
                    col.append(_chunk_head_outputs(st, qd, v_b, scores))
                    st = st * dec + upd
                st_ref[2 * p + hh] = st
                outs.append(col)
            return outs

        def stage4(p, z, s3):
            for hh in range(2):
                hd = 2 * p + hh
                for c in range(N_CHUNKS):
                    sl = chunk_head_slices(c, hh)
                    o = s3[hh][c]
                    o = o * lax.rsqrt(jnp.mean(o * o, axis=-1, keepdims=True) + RMS_EPS) * norm_w
                    hg_ref[sl[0], hd * HG_DK:(hd + 1) * HG_DK] = (
                        o * _silu(z[3][sl])).astype(_BF16)

        field = {}
        z, s1, s2 = {}, {}, {}
        fillers = (COL_V, COL_U, COL_GA, COL_GB)
        next_slice = FFN_LEAD
        for r in range(N_PAIRS + len(fillers)):
            if r < N_PAIRS:
                z[r] = _pair_project(u, w_in_ref, r)
            else:
                col = fillers[r - N_PAIRS]
                field[col] = _dot(u, _weight(w_in_ref[:, col:col + D_MODEL]))
            if next_slice < FF_SLICES:
                ffn_slice(next_slice)
                next_slice += 1
            p = r - 1
            if 0 <= p < N_PAIRS:
                lb = lb_ref[:, p * HEAD_PAIR:(p + 1) * HEAD_PAIR]
                s1[p] = _pair_gates(z[p][0], z[p][1], lb, tri)
            p = r - 2
            if 0 <= p < N_PAIRS:
                s2[p] = stage2(z[p], s1.pop(p))
            p = r - 3
            if 0 <= p < N_PAIRS:
                stage4(p, z.pop(p), stage3(p, s2.pop(p)))
        while next_slice < FF_SLICES:
            ffn_slice(next_slice)
            next_slice += 1

        gv = (_layer_norm(_gelu(field[COL_V])) * lnw_ref[...] + lnb_ref[...]).astype(_BF16)
        gu = _gelu(field[COL_U])
        n_blocks = TM // GM_BLOCK
        for g in range(GM_GROUPS):
            cols = slice(g * GM_CG, (g + 1) * GM_CG)
            rhs = jnp.concatenate([gv[nb * GM_BLOCK:(nb + 1) * GM_BLOCK, cols]
                                   for nb in range(n_blocks)], axis=1)
            sv = _dot(ws_ref[g], rhs)
            for nb in range(n_blocks):
                rows = slice(nb * GM_BLOCK, (nb + 1) * GM_BLOCK)
                sv_nb = sv[:, nb * GM_CG:(nb + 1) * GM_CG] + bs_ref[g]
                gm_ref[rows, cols] = (gu[rows, cols] * sv_nb).astype(_BF16)

        top, bot = slice(0, half), slice(half, TM)
        y_b = _dot(gm_ref[...], _weight(wpb_ref[...]))
        y_a = _dot(hg_ref[...], _weight(wpa_ref[...]))
        mix = (_sigmoid(field[COL_GA] + bgate_ref[0:1]) * y_a
               + _sigmoid(field[COL_GB] + bgate_ref[1:2]) * y_b).astype(_BF16)
        m = _dot(mix, _weight(wout_ref[...]))
        ffn_top = ffn_down(top)
        ffn_bot = ffn_down(bot)
        h1 = _layer_norm(DEEPNORM_ALPHA * x_ref[0] + g1 * m) * ln1w_ref[...] + ln1b_ref[...]
        h1_ref[slot] = h1
        sh2, sc2 = mod_ref[0, 3:4], mod_ref[0, 4:5]
        u2_ref[slot] = (_layer_norm(h1) * (1.0 + sc2) + sh2).astype(_BF16)
        ffn_finish(top, ffn_top)
        ffn_finish(bot, ffn_bot)

    tile_step(2 * s - 1, xa_ref, moda_ref, modf_ref, 0, 0)
    tile_step(2 * s, xb_ref, modb_ref, moda_ref, 1, 1)


def _block(x, mod, consts):
    n_tiles, _, d = x.shape
    n_batch = mod.shape[0]
    n_seq_tiles = n_tiles // n_batch
    last = n_tiles - 1
    assert n_tiles % 2 == 0 and n_seq_tiles % 2 == 0

    def const_spec(a):
        return pl.BlockSpec(a.shape, lambda s: (0,) * a.ndim, pipeline_mode=pl.Buffered(1))

    def tile_spec(offset):
        return pl.BlockSpec((1, TM, d), lambda s: (jnp.clip(2 * s + offset, 0, last), 0, 0))

    def mod_spec(offset):
        return pl.BlockSpec(
            (1, 6, d), lambda s: (jnp.clip(2 * s + offset, 0, last) // n_seq_tiles, 0, 0))

    return pl.pallas_call(
        functools.partial(_block_kernel, n_seq_tiles),
        grid=(n_tiles // 2 + 1,),
        in_specs=[tile_spec(-1), tile_spec(0), mod_spec(-1), mod_spec(0), mod_spec(-2)]
        + [const_spec(a) for a in consts],
        out_specs=pl.BlockSpec((2, TM, d), lambda s: (jnp.maximum(s - 1, 0), 0, 0)),
        out_shape=jax.ShapeDtypeStruct((n_tiles, TM, d), _F32),
        scratch_shapes=[
            pltpu.VMEM((HG_HEADS, HG_DK, HG_DK), _F32),
            pltpu.VMEM((TM, d), _BF16),
            pltpu.VMEM((TM, d), _BF16),
            pltpu.VMEM((2, TM, d), _F32),
            pltpu.VMEM((2, TM, d), _BF16),
            pltpu.VMEM((TM, D_FF), _BF16),
        ],
        compiler_params=pltpu.CompilerParams(
            dimension_semantics=("arbitrary",),
            vmem_limit_bytes=VMEM_LIMIT),
        name="block",
    )(x, x, mod, mod, mod, *consts)


def _chunk_tri(n):
    r = jnp.arange(n)[:, None]
    c = jnp.arange(n)[None, :]
    return ((r >= c) & (r // CHUNK == c // CHUNK)).astype(_BF16)


def kernel(x, c, w_ada, b_ada, w_in, b_gate, hgrn_lb_logits, hgrn_norm_w, w_proj_a, gmlp_ln_w,
           gmlp_ln_b, gmlp_ws, gmlp_bs, w_proj_b, w_out, ln1_w, ln1_b, w_ffn_in, w_ffn_out,
           ln2_w, ln2_b):
    assert w_ada.shape[0] == 1, "single-layer block"
    b, s, d = x.shape
    row = lambda a: a.reshape(1, -1)
    mod, lb = _prep(c, w_ada[0], b_ada[0], hgrn_lb_logits)
    mod = mod.transpose(1, 0, 2)

    pos = jnp.arange(GM_BLOCK) // CHUNK
    ws = jnp.where(pos[:, None] >= pos[None, :], gmlp_ws[0], 0.0).astype(_BF16)
    bs = jnp.broadcast_to(gmlp_bs[0][:, :, None], (GM_GROUPS, GM_BLOCK, GM_CG))

    p_in, p_a, p_b, p_out, p_ffn_in, p_ffn_out = _pack_weights(
        w_in[0], w_proj_a[0], w_proj_b[0], w_out[0], w_ffn_in[0], w_ffn_out[0])
    consts = (lb, p_in, _chunk_tri(TM), b_gate[0], row(hgrn_norm_w[0]),
              p_a, row(gmlp_ln_w[0]), row(gmlp_ln_b[0]), ws, bs,
              p_b, p_out, row(ln1_w[0]), row(ln1_b[0]),
              p_ffn_in, p_ffn_out, row(ln2_w[0]), row(ln2_b[0]))
    out = _block(x.reshape(b * s // TM, TM, d), mod, consts)
    return out.reshape(b, s, d)
```

```python
import functools

import jax
import jax.numpy as jnp
from jax import lax
from jax.experimental import pallas as pl
from jax.experimental.pallas import tpu as pltpu

D_MODEL = 1024
CHUNK = 64
SUB_CHUNK = 16
N_SUB = CHUNK // SUB_CHUNK
HG_DK = 128
HG_HEADS = D_MODEL // HG_DK
GM_BLOCK = 128
GM_GROUPS = 8
GM_CG = D_MODEL // GM_GROUPS
D_FF = 2816
DEEPNORM_ALPHA = 2.0 ** 0.25
LN_EPS = 1e-5
RMS_EPS = 1e-6

COL_Q, COL_F, COL_I, COL_G, COL_U, COL_V, COL_GA, COL_GB = (i * D_MODEL for i in range(8))

TM = 256
MXU_N = 256
HEAD_PAIR = MXU_N
N_CHUNKS = TM // CHUNK
N_PAIRS = HG_HEADS // 2
FF_SLICES = D_FF // MXU_N
FFN_LEAD = 4
FFN_FIRST_ROW = 1
PACK_STEPS = 8
VMEM_LIMIT = 62 * 1024 * 1024

_F32 = jnp.float32
_BF16 = jnp.bfloat16


def _dot(a, b):
    return jnp.dot(a, b, preferred_element_type=_F32)


def _dot_nt(a, b):
    return lax.dot_general(a, b, (((1,), (1,)), ((), ())), preferred_element_type=_F32)


def _dot_tn(a, b):
    return lax.dot_general(a, b, (((0,), (0,)), ((), ())), preferred_element_type=_F32)


def _pack_kernel(*refs):
    n = len(refs) // 2
    for w_ref, o_ref in zip(refs[:n], refs[n:]):
        o_ref[...] = pltpu.bitcast(w_ref[...].astype(_BF16), jnp.uint32)


def _pack_weights(*ws):
    for w in ws:
        assert w.shape[0] % (16 * PACK_STEPS) == 0, w.shape
    rows = [w.shape[0] // PACK_STEPS for w in ws]
    return pl.pallas_call(
        _pack_kernel,
        grid=(PACK_STEPS,),
        in_specs=[pl.BlockSpec((r, w.shape[1]), lambda i: (i, 0)) for r, w in zip(rows, ws)],
        out_specs=[pl.BlockSpec((r // 2, w.shape[1]), lambda i: (i, 0)) for r, w in zip(rows, ws)],
        out_shape=[jax.ShapeDtypeStruct((w.shape[0] // 2, w.shape[1]), jnp.uint32) for w in ws],
        compiler_params=pltpu.CompilerParams(vmem_limit_bytes=VMEM_LIMIT),
        name="pack",
    )(*ws)


def _weight(packed):
    return pltpu.bitcast(packed, _BF16)


def _layer_norm(x):
    xc = x - jnp.mean(x, axis=-1, keepdims=True)
    return xc * lax.rsqrt(jnp.mean(xc * xc, axis=-1, keepdims=True) + LN_EPS)


def _sigmoid(x):
    return 1.0 / (1.0 + jnp.exp(-x))


def _silu(x):
    return x * _sigmoid(x)


def _gelu(x):
    return 0.5 * x * (1.0 + lax.erf(x * (2.0 ** -0.5)))


def _prep_kernel(c_ref, w_ref, b_ref, lbl_ref, mod_ref, lb_ref):
    cond = _silu(c_ref[...])
    mod_ref[0] = _dot(cond.astype(_BF16), w_ref[...].astype(_BF16)) + b_ref[0]
    logits = lbl_ref[...]
    e = jnp.exp(logits - jnp.max(logits, axis=0, keepdims=True))
    lb_ref[...] = e[0:1] / jnp.sum(e, axis=0, keepdims=True)


def _prep(c, w_ada, b_ada, lb_logits):
    b = c.shape[0]
    return pl.pallas_call(
        _prep_kernel,
        grid=(6,),
        in_specs=[
            pl.BlockSpec((b, D_MODEL), lambda j: (0, 0)),
            pl.BlockSpec((D_MODEL, D_MODEL), lambda j: (0, j)),
            pl.BlockSpec((1, 1, D_MODEL), lambda j: (j, 0, 0)),
            pl.BlockSpec(lb_logits.shape, lambda j: (0, 0)),
        ],
        out_specs=[
            pl.BlockSpec((1, b, D_MODEL), lambda j: (j, 0, 0)),
            pl.BlockSpec((1, D_MODEL), lambda j: (0, 0)),
        ],
        out_shape=[
            jax.ShapeDtypeStruct((6, b, D_MODEL), _F32),
            jax.ShapeDtypeStruct((1, D_MODEL), _F32),
        ],
        name="prep",
    )(c, w_ada, b_ada.reshape(6, 1, D_MODEL), lb_logits)


def _pair_project(u, w_in_ref, p):
    off = p * HEAD_PAIR
    zq, zf, zi, zg = (_dot(u, _weight(w_in_ref[:, col + off:col + off + HEAD_PAIR]))
                      for col in (COL_Q, COL_F, COL_I, COL_G))
    return zq, zf, zi.astype(_BF16), zg


def _pair_gates(zq, zf, lb, tri):
    f = lb + (1.0 - lb) * _sigmoid(zf)
    l2f = jnp.log2(f)
    hi = l2f.astype(_BF16)
    lo = (l2f - hi.astype(_F32)).astype(_BF16)
    cum = _dot(tri, hi) + _dot(tri, lo)
    return _silu(zq), 1.0 - f, cum


def _chunk_head_scores(q, k, cum, v_b):
    cl = cum[CHUNK - 1:CHUNK]
    qd = (q * jnp.exp2(cum)).astype(_BF16)
    kd = (k * jnp.exp2(cl - cum)).astype(_BF16)
    upd = _dot_tn(kd, v_b)
    dec = jnp.transpose(jnp.broadcast_to(jnp.exp2(cl), (HG_DK, HG_DK)))

    def keys(ref, hi):
        return (k[0:hi] * jnp.exp2(-cum[0:hi] if ref is None else ref - cum[0:hi])).astype(_BF16)

    def queries(ref, lo):
        return (q[lo:lo + SUB_CHUNK] * jnp.exp2(cum[lo:lo + SUB_CHUNK] - ref)).astype(_BF16)

    zeros = jnp.zeros((SUB_CHUNK, HG_DK), _BF16)
    scores = []
    for i0 in range(0, N_SUB, 2):
        lo0, lo1 = i0 * SUB_CHUNK, (i0 + 1) * SUB_CHUNK
        hi = lo1 + SUB_CHUNK
        ref0 = None if i0 == 0 else cum[lo0 - 1:lo0]
        ref1 = cum[lo1 - 1:lo1]
        q0 = qd[lo0:lo1] if i0 == 0 else queries(ref0, lo0)
        q1 = queries(ref1, lo1)
        lhs = jnp.concatenate([jnp.concatenate([q0, zeros], axis=1),
                               jnp.concatenate([zeros, q1], axis=1)], axis=0)
        rhs = jnp.concatenate([jnp.concatenate([keys(ref0, lo1), zeros], axis=0),
                               keys(ref1, hi)], axis=1)
        scores.append(_dot_nt(lhs, rhs))
    return qd, v_b, dec, upd, scores


def _chunk_head_outputs(st, qd, v_b, scores):
    o_inter = _dot(qd, st.astype(_BF16))
    outs = []
    for j, sc in enumerate(scores):
        lo = 2 * j * SUB_CHUNK
        row = lax.broadcasted_iota(jnp.int32, sc.shape, 0) + lo
        col = lax.broadcasted_iota(jnp.int32, sc.shape, 1)
        sc = jnp.where(col <= row, sc, 0.0).astype(_BF16)
        outs.append(_dot(sc, v_b[0:sc.shape[1]]))
    return o_inter + jnp.concatenate(outs, axis=0)


def _block_kernel(n_seq_tiles,
                  xa_ref, xb_ref, moda_ref, modb_ref, modf_ref, lb_ref, w_in_ref, tri_ref, bgate_ref,
                  normw_ref, wpa_ref, lnw_ref, lnb_ref, ws_ref, bs_ref, wpb_ref, wout_ref, ln1w_ref,
                  ln1b_ref, w1_ref, w2_ref, ln2w_ref, ln2b_ref,
                  out_ref, st_ref, hg_ref, gm_ref, h1_ref, u2_ref, act_ref):
    s = pl.program_id(0)

    @pl.when(s == 0)
    def _():
        st_ref[...] = jnp.zeros_like(st_ref)
        h1_ref[...] = jnp.zeros_like(h1_ref)
        u2_ref[...] = jnp.zeros_like(u2_ref)

    half = TM // 2
    tri = tri_ref[...]
    norm_w = normw_ref[...]

    def chunk_head_slices(c, hh):
        return (slice(c * CHUNK, (c + 1) * CHUNK), slice(hh * HG_DK, (hh + 1) * HG_DK))

    def tile_step(tile, x_ref, mod_ref, modf_ref, slot, out_idx):
        prev = 1 - slot
        keep = jnp.where(tile % n_seq_tiles == 0, 0.0, 1.0).astype(_F32)

        def ffn_slice(i):
            u2 = u2_ref[prev]
            a = _dot(u2, _weight(w1_ref[:, i * MXU_N:(i + 1) * MXU_N]))
            b = _dot(u2, _weight(w1_ref[:, D_FF + i * MXU_N:D_FF + (i + 1) * MXU_N]))
            act_ref[:, i * MXU_N:(i + 1) * MXU_N] = (_silu(a) * b).astype(_BF16)

        def ffn_down(rows):
            return _dot(act_ref[rows, :], _weight(w2_ref[...]))

        def ffn_finish(rows, ffn):
            g2 = modf_ref[0, 5:6]
            r = DEEPNORM_ALPHA * h1_ref[prev, rows, :] + g2 * ffn
            out_ref[out_idx, rows, :] = _layer_norm(r) * ln2w_ref[...] + ln2b_ref[...]

        sh1, sc1, g1 = mod_ref[0, 0:1], mod_ref[0, 1:2], mod_ref[0, 2:3]
        for i in range(FFN_LEAD):
            ffn_slice(i)
        u = (_layer_norm(x_ref[0]) * (1.0 + sc1) + sh1).astype(_BF16)

        def stage2(z, gates):
            q, k, cum = gates
            return [[_chunk_head_scores(q[sl], k[sl], cum[sl], z[2][sl])
                     for sl in (chunk_head_slices(c, hh) for hh in range(2))]
                    for c in range(N_CHUNKS)]

        def stage3(p, s2):
            outs = []
            for hh in range(2):
                st = st_ref[2 * p + hh] * keep
                col = []
                for c in range(N_CHUNKS):
                    qd, v_b, dec, upd, scores = s2[c][hh]
                    col.append(_chunk_head_outputs(st, qd, v_b, scores))
                    st = st * dec + upd
                st_ref[2 * p + hh] = st
                outs.append(col)
            return outs

        def stage4(p, z, s3):
            for hh in range(2):
                hd = 2 * p + hh
                for c in range(N_CHUNKS):
                    sl = chunk_head_slices(c, hh)
                    o = s3[hh][c]
                    o = o * lax.rsqrt(jnp.mean(o * o, axis=-1, keepdims=True) + RMS_EPS) * norm_w
                    hg_ref[sl[0], hd * HG_DK:(hd + 1) * HG_DK] = (
                        o * _silu(z[3][sl])).astype(_BF16)

        field = {}
        z, s1, s2 = {}, {}, {}
        fillers = (COL_V, COL_U, COL_GA, COL_GB)
        next_slice = FFN_LEAD
        for r in range(N_PAIRS + len(fillers)):
            if r < N_PAIRS:
                z[r] = _pair_project(u, w_in_ref, r)
            else:
                col = fillers[r - N_PAIRS]
                field[col] = _dot(u, _weight(w_in_ref[:, col:col + D_MODEL]))
            if r >= FFN_FIRST_ROW and next_slice < FF_SLICES:
                ffn_slice(next_slice)
                next_slice += 1
            p = r - 1
            if 0 <= p < N_PAIRS:
                lb = lb_ref[:, p * HEAD_PAIR:(p + 1) * HEAD_PAIR]
                s1[p] = _pair_gates(z[p][0], z[p][1], lb, tri)
            p = r - 2
            if 0 <= p < N_PAIRS:
                s2[p] = stage2(z[p], s1.pop(p))
            p = r - 3
            if 0 <= p < N_PAIRS:
                stage4(p, z.pop(p), stage3(p, s2.pop(p)))
        while next_slice < FF_SLICES:
            ffn_slice(next_slice)
            next_slice += 1

        gv = (_layer_norm(_gelu(field[COL_V])) * lnw_ref[...] + lnb_ref[...]).astype(_BF16)
        gu = _gelu(field[COL_U])
        n_blocks = TM // GM_BLOCK
        for g in range(GM_GROUPS):
            cols = slice(g * GM_CG, (g + 1) * GM_CG)
            rhs = jnp.concatenate([gv[nb * GM_BLOCK:(nb + 1) * GM_BLOCK, cols]
                                   for nb in range(n_blocks)], axis=1)
            sv = _dot(ws_ref[g], rhs)
            for nb in range(n_blocks):
                rows = slice(nb * GM_BLOCK, (nb + 1) * GM_BLOCK)
                sv_nb = sv[:, nb * GM_CG:(nb + 1) * GM_CG] + bs_ref[g]
                gm_ref[rows, cols] = (gu[rows, cols] * sv_nb).astype(_BF16)

        top, bot = slice(0, half), slice(half, TM)
        y_b = _dot(gm_ref[...], _weight(wpb_ref[...]))
        y_a = _dot(hg_ref[...], _weight(wpa_ref[...]))
        mix = (_sigmoid(field[COL_GA] + bgate_ref[0:1]) * y_a
               + _sigmoid(field[COL_GB] + bgate_ref[1:2]) * y_b).astype(_BF16)
        m = _dot(mix, _weight(wout_ref[...]))
        ffn_top = ffn_down(top)
        ffn_bot = ffn_down(bot)
        h1 = _layer_norm(DEEPNORM_ALPHA * x_ref[0] + g1 * m) * ln1w_ref[...] + ln1b_ref[...]
        h1_ref[slot] = h1
        sh2, sc2 = mod_ref[0, 3:4], mod_ref[0, 4:5]
        u2_ref[slot] = (_layer_norm(h1) * (1.0 + sc2) + sh2).astype(_BF16)
        ffn_finish(top, ffn_top)
        ffn_finish(bot, ffn_bot)

    tile_step(2 * s - 1, xa_ref, moda_ref, modf_ref, 0, 0)
    tile_step(2 * s, xb_ref, modb_ref, moda_ref, 1, 1)


def _block(x, mod, consts):
    n_tiles, _, d = x.shape
    n_batch = mod.shape[0]
    n_seq_tiles = n_tiles // n_batch
    last = n_tiles - 1
    assert n_tiles % 2 == 0 and n_seq_tiles % 2 == 0

    def const_spec(a):
        return pl.BlockSpec(a.shape, lambda s: (0,) * a.ndim, pipeline_mode=pl.Buffered(1))

    def tile_spec(offset):
        return pl.BlockSpec((1, TM, d), lambda s: (jnp.clip(2 * s + offset, 0, last), 0, 0))

    def mod_spec(offset):
        return pl.BlockSpec(
            (1, 6, d), lambda s: (jnp.clip(2 * s + offset, 0, last) // n_seq_tiles, 0, 0))

    return pl.pallas_call(
        functools.partial(_block_kernel, n_seq_tiles),
        grid=(n_tiles // 2 + 1,),
        in_specs=[tile_spec(-1), tile_spec(0), mod_spec(-1), mod_spec(0), mod_spec(-2)]
        + [const_spec(a) for a in consts],
        out_specs=pl.BlockSpec((2, TM, d), lambda s: (jnp.maximum(s - 1, 0), 0, 0)),
        out_shape=jax.ShapeDtypeStruct((n_tiles, TM, d), _F32),
        scratch_shapes=[
            pltpu.VMEM((HG_HEADS, HG_DK, HG_DK), _F32),
            pltpu.VMEM((TM, d), _BF16),
            pltpu.VMEM((TM, d), _BF16),
            pltpu.VMEM((2, TM, d), _F32),
            pltpu.VMEM((2, TM, d), _BF16),
            pltpu.VMEM((TM, D_FF), _BF16),
        ],
        compiler_params=pltpu.CompilerParams(
            dimension_semantics=("arbitrary",),
            vmem_limit_bytes=VMEM_LIMIT),
        name="block",
    )(x, x, mod, mod, mod, *consts)


def _chunk_tri(n):
    r = jnp.arange(n)[:, None]
    c = jnp.arange(n)[None, :]
    return ((r >= c) & (r // CHUNK == c // CHUNK)).astype(_BF16)


def kernel(x, c, w_ada, b_ada, w_in, b_gate, hgrn_lb_logits, hgrn_norm_w, w_proj_a, gmlp_ln_w,
           gmlp_ln_b, gmlp_ws, gmlp_bs, w_proj_b, w_out, ln1_w, ln1_b, w_ffn_in, w_ffn_out,
           ln2_w, ln2_b):
    assert w_ada.shape[0] == 1, "single-layer block"
    b, s, d = x.shape
    row = lambda a: a.reshape(1, -1)
    mod, lb = _prep(c, w_ada[0], b_ada[0], hgrn_lb_logits)
    mod = mod.transpose(1, 0, 2)

    pos = jnp.arange(GM_BLOCK) // CHUNK
    ws = jnp.where(pos[:, None] >= pos[None, :], gmlp_ws[0], 0.0).astype(_BF16)
    bs = jnp.broadcast_to(gmlp_bs[0][:, :, None], (GM_GROUPS, GM_BLOCK, GM_CG))

    p_in, p_a, p_b, p_out, p_ffn_in, p_ffn_out = _pack_weights(
        w_in[0], w_proj_a[0], w_proj_b[0], w_out[0], w_ffn_in[0], w_ffn_out[0])
    consts = (lb, p_in, _chunk_tri(TM), b_gate[0], row(hgrn_norm_w[0]),
              p_a, row(gmlp_ln_w[0]), row(gmlp_ln_b[0]), ws, bs,
              p_b, p_out, row(ln1_w[0]), row(ln1_b[0]),
              p_ffn_in, p_ffn_out, row(ln2_w[0]), row(ln2_b[0]))
    out = _block(x.reshape(b * s // TM, TM, d), mod, consts)
    return out.reshape(b, s, d)
```

```python
import functools

import jax
import jax.numpy as jnp
from jax import lax
from jax.experimental import pallas as pl
from jax.experimental.pallas import tpu as pltpu

D_MODEL = 1024
CHUNK = 64
SUB_CHUNK = 16
N_SUB = CHUNK // SUB_CHUNK
HG_DK = 128
HG_HEADS = D_MODEL // HG_DK
GM_BLOCK = 128
GM_GROUPS = 8
GM_CG = D_MODEL // GM_GROUPS
D_FF = 2816
DEEPNORM_ALPHA = 2.0 ** 0.25
LN_EPS = 1e-5
RMS_EPS = 1e-6

COL_Q, COL_F, COL_I, COL_G, COL_U, COL_V, COL_GA, COL_GB = (i * D_MODEL for i in range(8))

TM = 256
MXU_N = 256
HEAD_PAIR = MXU_N
N_CHUNKS = TM // CHUNK
N_PAIRS = HG_HEADS // 2
FF_SLICES = D_FF // MXU_N
FFN_LEAD = 4
FFN_PER_ROW = 2
PACK_STEPS = 8
VMEM_LIMIT = 62 * 1024 * 1024

_F32 = jnp.float32
_BF16 = jnp.bfloat16


def _dot(a, b):
    return jnp.dot(a, b, preferred_element_type=_F32)


def _dot_nt(a, b):
    return lax.dot_general(a, b, (((1,), (1,)), ((), ())), preferred_element_type=_F32)


def _dot_tn(a, b):
    return lax.dot_general(a, b, (((0,), (0,)), ((), ())), preferred_element_type=_F32)


def _pack_kernel(*refs):
    n = len(refs) // 2
    for w_ref, o_ref in zip(refs[:n], refs[n:]):
        o_ref[...] = pltpu.bitcast(w_ref[...].astype(_BF16), jnp.uint32)


def _pack_weights(*ws):
    for w in ws:
        assert w.shape[0] % (16 * PACK_STEPS) == 0, w.shape
    rows = [w.shape[0] // PACK_STEPS for w in ws]
    return pl.pallas_call(
        _pack_kernel,
        grid=(PACK_STEPS,),
        in_specs=[pl.BlockSpec((r, w.shape[1]), lambda i: (i, 0)) for r, w in zip(rows, ws)],
        out_specs=[pl.BlockSpec((r // 2, w.shape[1]), lambda i: (i, 0)) for r, w in zip(rows, ws)],
        out_shape=[jax.ShapeDtypeStruct((w.shape[0] // 2, w.shape[1]), jnp.uint32) for w in ws],
        compiler_params=pltpu.CompilerParams(vmem_limit_bytes=VMEM_LIMIT),
        name="pack",
    )(*ws)


def _weight(packed):
    return pltpu.bitcast(packed, _BF16)


def _layer_norm(x):
    xc = x - jnp.mean(x, axis=-1, keepdims=True)
    return xc * lax.rsqrt(jnp.mean(xc * xc, axis=-1, keepdims=True) + LN_EPS)


def _sigmoid(x):
    return 1.0 / (1.0 + jnp.exp(-x))


def _silu(x):
    return x * _sigmoid(x)


def _gelu(x):
    return 0.5 * x * (1.0 + lax.erf(x * (2.0 ** -0.5)))


def _prep_kernel(c_ref, w_ref, b_ref, lbl_ref, mod_ref, lb_ref):
    cond = _silu(c_ref[...])
    mod_ref[0] = _dot(cond.astype(_BF16), w_ref[...].astype(_BF16)) + b_ref[0]
    logits = lbl_ref[...]
    e = jnp.exp(logits - jnp.max(logits, axis=0, keepdims=True))
    lb_ref[...] = e[0:1] / jnp.sum(e, axis=0, keepdims=True)


def _prep(c, w_ada, b_ada, lb_logits):
    b = c.shape[0]
    return pl.pallas_call(
        _prep_kernel,
        grid=(6,),
        in_specs=[
            pl.BlockSpec((b, D_MODEL), lambda j: (0, 0)),
            pl.BlockSpec((D_MODEL, D_MODEL), lambda j: (0, j)),
            pl.BlockSpec((1, 1, D_MODEL), lambda j: (j, 0, 0)),
            pl.BlockSpec(lb_logits.shape, lambda j: (0, 0)),
        ],
        out_specs=[
            pl.BlockSpec((1, b, D_MODEL), lambda j: (j, 0, 0)),
            pl.BlockSpec((1, D_MODEL), lambda j: (0, 0)),
        ],
        out_shape=[
            jax.ShapeDtypeStruct((6, b, D_MODEL), _F32),
            jax.ShapeDtypeStruct((1, D_MODEL), _F32),
        ],
        name="prep",
    )(c, w_ada, b_ada.reshape(6, 1, D_MODEL), lb_logits)


def _pair_project(u, w_in_ref, p):
    off = p * HEAD_PAIR
    zq, zf, zi, zg = (_dot(u, _weight(w_in_ref[:, col + off:col + off + HEAD_PAIR]))
                      for col in (COL_Q, COL_F, COL_I, COL_G))
    return zq, zf, zi.astype(_BF16), zg


def _pair_gates(zq, zf, lb, tri):
    f = lb + (1.0 - lb) * _sigmoid(zf)
    l2f = jnp.log2(f)
    hi = l2f.astype(_BF16)
    lo = (l2f - hi.astype(_F32)).astype(_BF16)
    cum = _dot(tri, hi) + _dot(tri, lo)
    return _silu(zq), 1.0 - f, cum


def _chunk_head_scores(q, k, cum, v_b):
    cl = cum[CHUNK - 1:CHUNK]
    qd = (q * jnp.exp2(cum)).astype(_BF16)
    kd = (k * jnp.exp2(cl - cum)).astype(_BF16)
    upd = _dot_tn(kd, v_b)
    dec = jnp.transpose(jnp.broadcast_to(jnp.exp2(cl), (HG_DK, HG_DK)))

    def keys(ref, hi):
        return (k[0:hi] * jnp.exp2(-cum[0:hi] if ref is None else ref - cum[0:hi])).astype(_BF16)

    def queries(ref, lo):
        return (q[lo:lo + SUB_CHUNK] * jnp.exp2(cum[lo:lo + SUB_CHUNK] - ref)).astype(_BF16)

    zeros = jnp.zeros((SUB_CHUNK, HG_DK), _BF16)
    scores = []
    for i0 in range(0, N_SUB, 2):
        lo0, lo1 = i0 * SUB_CHUNK, (i0 + 1) * SUB_CHUNK
        hi = lo1 + SUB_CHUNK
        ref0 = None if i0 == 0 else cum[lo0 - 1:lo0]
        ref1 = cum[lo1 - 1:lo1]
        q0 = qd[lo0:lo1] if i0 == 0 else queries(ref0, lo0)
        q1 = queries(ref1, lo1)
        lhs = jnp.concatenate([jnp.concatenate([q0, zeros], axis=1),
                               jnp.concatenate([zeros, q1], axis=1)], axis=0)
        rhs = jnp.concatenate([jnp.concatenate([keys(ref0, lo1), zeros], axis=0),
                               keys(ref1, hi)], axis=1)
        scores.append(_dot_nt(lhs, rhs))
    return qd, v_b, dec, upd, scores


def _chunk_head_outputs(st, qd, v_b, scores):
    o_inter = _dot(qd, st.astype(_BF16))
    outs = []
    for j, sc in enumerate(scores):
        lo = 2 * j * SUB_CHUNK
        row = lax.broadcasted_iota(jnp.int32, sc.shape, 0) + lo
        col = lax.broadcasted_iota(jnp.int32, sc.shape, 1)
        sc = jnp.where(col <= row, sc, 0.0).astype(_BF16)
        outs.append(_dot(sc, v_b[0:sc.shape[1]]))
    return o_inter + jnp.concatenate(outs, axis=0)


def _block_kernel(n_seq_tiles,
                  xa_ref, xb_ref, moda_ref, modb_ref, modf_ref, lb_ref, w_in_ref, tri_ref, bgate_ref,
                  normw_ref, wpa_ref, lnw_ref, lnb_ref, ws_ref, bs_ref, wpb_ref, wout_ref, ln1w_ref,
                  ln1b_ref, w1_ref, w2_ref, ln2w_ref, ln2b_ref,
                  out_ref, st_ref, hg_ref, gm_ref, h1_ref, u2_ref, act_ref):
    s = pl.program_id(0)

    @pl.when(s == 0)
    def _():
        st_ref[...] = jnp.zeros_like(st_ref)
        h1_ref[...] = jnp.zeros_like(h1_ref)
        u2_ref[...] = jnp.zeros_like(u2_ref)

    half = TM // 2
    tri = tri_ref[...]
    norm_w = normw_ref[...]

    def chunk_head_slices(c, hh):
        return (slice(c * CHUNK, (c + 1) * CHUNK), slice(hh * HG_DK, (hh + 1) * HG_DK))

    def tile_step(tile, x_ref, mod_ref, modf_ref, slot, out_idx):
        prev = 1 - slot
        keep = jnp.where(tile % n_seq_tiles == 0, 0.0, 1.0).astype(_F32)

        def ffn_slice(i):
            u2 = u2_ref[prev]
            a = _dot(u2, _weight(w1_ref[:, i * MXU_N:(i + 1) * MXU_N]))
            b = _dot(u2, _weight(w1_ref[:, D_FF + i * MXU_N:D_FF + (i + 1) * MXU_N]))
            act_ref[:, i * MXU_N:(i + 1) * MXU_N] = (_silu(a) * b).astype(_BF16)

        def ffn_down(rows):
            return _dot(act_ref[rows, :], _weight(w2_ref[...]))

        def ffn_finish(rows, ffn):
            g2 = modf_ref[0, 5:6]
            r = DEEPNORM_ALPHA * h1_ref[prev, rows, :] + g2 * ffn
            out_ref[out_idx, rows, :] = _layer_norm(r) * ln2w_ref[...] + ln2b_ref[...]

        sh1, sc1, g1 = mod_ref[0, 0:1], mod_ref[0, 1:2], mod_ref[0, 2:3]
        for i in range(FFN_LEAD):
            ffn_slice(i)
        u = (_layer_norm(x_ref[0]) * (1.0 + sc1) + sh1).astype(_BF16)

        def stage2(z, gates):
            q, k, cum = gates
            return [[_chunk_head_scores(q[sl], k[sl], cum[sl], z[2][sl])
                     for sl in (chunk_head_slices(c, hh) for hh in range(2))]
                    for c in range(N_CHUNKS)]

        def stage3(p, s2):
            outs = []
            for hh in range(2):
                st = st_ref[2 * p + hh] * keep
                col = []
                for c in range(N_CHUNKS):
                    qd, v_b, dec, upd, scores = s2[c][hh]
                    col.append(_chunk_head_outputs(st, qd, v_b, scores))
                    st = st * dec + upd
                st_ref[2 * p + hh] = st
                outs.append(col)
            return outs

        def stage4(p, z, s3):
            for hh in range(2):
                hd = 2 * p + hh
                for c in range(N_CHUNKS):
                    sl = chunk_head_slices(c, hh)
                    o = s3[hh][c]
                    o = o * lax.rsqrt(jnp.mean(o * o, axis=-1, keepdims=True) + RMS_EPS) * norm_w
                    hg_ref[sl[0], hd * HG_DK:(hd + 1) * HG_DK] = (
                        o * _silu(z[3][sl])).astype(_BF16)

        field = {}
        z, s1, s2 = {}, {}, {}
        fillers = (COL_V, COL_U, COL_GA, COL_GB)
        next_slice = FFN_LEAD
        for r in range(N_PAIRS + len(fillers)):
            if r < N_PAIRS:
                z[r] = _pair_project(u, w_in_ref, r)
            else:
                col = fillers[r - N_PAIRS]
                field[col] = _dot(u, _weight(w_in_ref[:, col:col + D_MODEL]))
            for _ in range(FFN_PER_ROW):
                if next_slice < FF_SLICES:
                    ffn_slice(next_slice)
                    next_slice += 1
            p = r - 1
            if 0 <= p < N_PAIRS:
                lb = lb_ref[:, p * HEAD_PAIR:(p + 1) * HEAD_PAIR]
                s1[p] = _pair_gates(z[p][0], z[p][1], lb, tri)
            p = r - 2
            if 0 <= p < N_PAIRS:
                s2[p] = stage2(z[p], s1.pop(p))
            p = r - 3
            if 0 <= p < N_PAIRS:
                stage4(p, z.pop(p), stage3(p, s2.pop(p)))
        while next_slice < FF_SLICES:
            ffn_slice(next_slice)
            next_slice += 1

        gv = (_layer_norm(_gelu(field[COL_V])) * lnw_ref[...] + lnb_ref[...]).astype(_BF16)
        gu = _gelu(field[COL_U])
        n_blocks = TM // GM_BLOCK
        for g in range(GM_GROUPS):
            cols = slice(g * GM_CG, (g + 1) * GM_CG)
            rhs = jnp.concatenate([gv[nb * GM_BLOCK:(nb + 1) * GM_BLOCK, cols]
                                   for nb in range(n_blocks)], axis=1)
            sv = _dot(ws_ref[g], rhs)
            for nb in range(n_blocks):
                rows = slice(nb * GM_BLOCK, (nb + 1) * GM_BLOCK)
                sv_nb = sv[:, nb * GM_CG:(nb + 1) * GM_CG] + bs_ref[g]
                gm_ref[rows, cols] = (gu[rows, cols] * sv_nb).astype(_BF16)

        top, bot = slice(0, half), slice(half, TM)
        y_b = _dot(gm_ref[...], _weight(wpb_ref[...]))
        y_a = _dot(hg_ref[...], _weight(wpa_ref[...]))
        mix = (_sigmoid(field[COL_GA] + bgate_ref[0:1]) * y_a
               + _sigmoid(field[COL_GB] + bgate_ref[1:2]) * y_b).astype(_BF16)
        m = _dot(mix, _weight(wout_ref[...]))
        ffn_top = ffn_down(top)
        ffn_bot = ffn_down(bot)
        h1 = _layer_norm(DEEPNORM_ALPHA * x_ref[0] + g1 * m) * ln1w_ref[...] + ln1b_ref[...]
        h1_ref[slot] = h1
        sh2, sc2 = mod_ref[0, 3:4], mod_ref[0, 4:5]
        u2_ref[slot] = (_layer_norm(h1) * (1.0 + sc2) + sh2).astype(_BF16)
        ffn_finish(top, ffn_top)
        ffn_finish(bot, ffn_bot)

    tile_step(2 * s - 1, xa_ref, moda_ref, modf_ref, 0, 0)
    tile_step(2 * s, xb_ref, modb_ref, moda_ref, 1, 1)


def _block(x, mod, consts):
    n_tiles, _, d = x.shape
    n_batch = mod.shape[0]
    n_seq_tiles = n_tiles // n_batch
    last = n_tiles - 1
    assert n_tiles % 2 == 0 and n_seq_tiles % 2 == 0

    def const_spec(a):
        return pl.BlockSpec(a.shape, lambda s: (0,) * a.ndim, pipeline_mode=pl.Buffered(1))

    def tile_spec(offset):
        return pl.BlockSpec((1, TM, d), lambda s: (jnp.clip(2 * s + offset, 0, last), 0, 0))

    def mod_spec(offset):
        return pl.BlockSpec(
            (1, 6, d), lambda s: (jnp.clip(2 * s + offset, 0, last) // n_seq_tiles, 0, 0))

    return pl.pallas_call(
        functools.partial(_block_kernel, n_seq_tiles),
        grid=(n_tiles // 2 + 1,),
        in_specs=[tile_spec(-1), tile_spec(0), mod_spec(-1), mod_spec(0), mod_spec(-2)]
        + [const_spec(a) for a in consts],
        out_specs=pl.BlockSpec((2, TM, d), lambda s: (jnp.maximum(s - 1, 0), 0, 0)),
        out_shape=jax.ShapeDtypeStruct((n_tiles, TM, d), _F32),
        scratch_shapes=[
            pltpu.VMEM((HG_HEADS, HG_DK, HG_DK), _F32),
            pltpu.VMEM((TM, d), _BF16),
            pltpu.VMEM((TM, d), _BF16),
            pltpu.VMEM((2, TM, d), _F32),
            pltpu.VMEM((2, TM, d), _BF16),
            pltpu.VMEM((TM, D_FF), _BF16),
        ],
        compiler_params=pltpu.CompilerParams(
            dimension_semantics=("arbitrary",),
            vmem_limit_bytes=VMEM_LIMIT),
        name="block",
    )(x, x, mod, mod, mod, *consts)


def _chunk_tri(n):
    r = jnp.arange(n)[:, None]
    c = jnp.arange(n)[None, :]
    return ((r >= c) & (r // CHUNK == c // CHUNK)).astype(_BF16)


def kernel(x, c, w_ada, b_ada, w_in, b_gate, hgrn_lb_logits, hgrn_norm_w, w_proj_a, gmlp_ln_w,
           gmlp_ln_b, gmlp_ws, gmlp_bs, w_proj_b, w_out, ln1_w, ln1_b, w_ffn_in, w_ffn_out,
           ln2_w, ln2_b):
    assert w_ada.shape[0] == 1, "single-layer block"
    b, s, d = x.shape
    row = lambda a: a.reshape(1, -1)
    mod, lb = _prep(c, w_ada[0], b_ada[0], hgrn_lb_logits)
    mod = mod.transpose(1, 0, 2)

    pos = jnp.arange(GM_BLOCK) // CHUNK
    ws = jnp.where(pos[:, None] >= pos[None, :], gmlp_ws[0], 0.0).astype(_BF16)
    bs = jnp.broadcast_to(gmlp_bs[0][:, :, None], (GM_GROUPS, GM_BLOCK, GM_CG))

    p_in, p_a, p_b, p_out, p_ffn_in, p_ffn_out = _pack_weights(
        w_in[0], w_proj_a[0], w_proj_b[0], w_out[0], w_ffn_in[0], w_ffn_out[0])
    consts = (lb, p_in, _chunk_tri(TM), b_gate[0], row(hgrn_norm_w[0]),
              p_a, row(gmlp_ln_w[0]), row(gmlp_ln_b[0]), ws, bs,
              p_b, p_out, row(ln1_w[0]), row(ln1_b[0]),
              p_ffn_in, p_ffn_out, row(ln2_w[0]), row(ln2_b[0]))
    out = _block(x.reshape(b * s // TM, TM, d), mod, consts)
    return out.reshape(b, s, d)
```

```python
import functools

import jax
import jax.numpy as jnp
from jax import lax
from jax.experimental import pallas as pl
from jax.experimental.pallas import tpu as pltpu

D_MODEL = 1024
CHUNK = 64
SUB_CHUNK = 16
N_SUB = CHUNK // SUB_CHUNK
HG_DK = 128
HG_HEADS = D_MODEL // HG_DK
GM_BLOCK = 128
GM_GROUPS = 8
GM_CG = D_MODEL // GM_GROUPS
D_FF = 2816
DEEPNORM_ALPHA = 2.0 ** 0.25
LN_EPS = 1e-5
RMS_EPS = 1e-6

COL_Q, COL_F, COL_I, COL_G, COL_U, COL_V, COL_GA, COL_GB = (i * D_MODEL for i in range(8))

TM = 256
MXU_N = 256
HEAD_PAIR = MXU_N
N_CHUNKS = TM // CHUNK
N_PAIRS = HG_HEADS // 2
FF_SLICES = D_FF // MXU_N
FFN_LEAD = 4
FFN_PER_ROW = 1
PACK_STEPS = 8
VMEM_LIMIT = 63 * 1024 * 1024 + 512 * 1024

_F32 = jnp.float32
_BF16 = jnp.bfloat16


def _dot(a, b):
    return jnp.dot(a, b, preferred_element_type=_F32)


def _dot_nt(a, b):
    return lax.dot_general(a, b, (((1,), (1,)), ((), ())), preferred_element_type=_F32)


def _dot_tn(a, b):
    return lax.dot_general(a, b, (((0,), (0,)), ((), ())), preferred_element_type=_F32)


def _pack_kernel(*refs):
    n = len(refs) // 2
    for w_ref, o_ref in zip(refs[:n], refs[n:]):
        o_ref[...] = pltpu.bitcast(w_ref[...].astype(_BF16), jnp.uint32)


def _pack_weights(*ws):
    for w in ws:
        assert w.shape[0] % (16 * PACK_STEPS) == 0, w.shape
    rows = [w.shape[0] // PACK_STEPS for w in ws]
    return pl.pallas_call(
        _pack_kernel,
        grid=(PACK_STEPS,),
        in_specs=[pl.BlockSpec((r, w.shape[1]), lambda i: (i, 0)) for r, w in zip(rows, ws)],
        out_specs=[pl.BlockSpec((r // 2, w.shape[1]), lambda i: (i, 0)) for r, w in zip(rows, ws)],
        out_shape=[jax.ShapeDtypeStruct((w.shape[0] // 2, w.shape[1]), jnp.uint32) for w in ws],
        compiler_params=pltpu.CompilerParams(vmem_limit_bytes=VMEM_LIMIT),
        name="pack",
    )(*ws)


def _weight(packed):
    return pltpu.bitcast(packed, _BF16)


def _layer_norm(x):
    xc = x - jnp.mean(x, axis=-1, keepdims=True)
    return xc * lax.rsqrt(jnp.mean(xc * xc, axis=-1, keepdims=True) + LN_EPS)


def _sigmoid(x):
    return 1.0 / (1.0 + jnp.exp(-x))


def _silu(x):
    return x * _sigmoid(x)


def _gelu(x):
    return 0.5 * x * (1.0 + lax.erf(x * (2.0 ** -0.5)))


def _prep_kernel(c_ref, w_ref, b_ref, lbl_ref, mod_ref, lb_ref):
    cond = _silu(c_ref[...])
    mod_ref[0] = _dot(cond.astype(_BF16), w_ref[...].astype(_BF16)) + b_ref[0]
    logits = lbl_ref[...]
    e = jnp.exp(logits - jnp.max(logits, axis=0, keepdims=True))
    lb_ref[...] = e[0:1] / jnp.sum(e, axis=0, keepdims=True)


def _prep(c, w_ada, b_ada, lb_logits):
    b = c.shape[0]
    return pl.pallas_call(
        _prep_kernel,
        grid=(6,),
        in_specs=[
            pl.BlockSpec((b, D_MODEL), lambda j: (0, 0)),
            pl.BlockSpec((D_MODEL, D_MODEL), lambda j: (0, j)),
            pl.BlockSpec((1, 1, D_MODEL), lambda j: (j, 0, 0)),
            pl.BlockSpec(lb_logits.shape, lambda j: (0, 0)),
        ],
        out_specs=[
            pl.BlockSpec((1, b, D_MODEL), lambda j: (j, 0, 0)),
            pl.BlockSpec((1, D_MODEL), lambda j: (0, 0)),
        ],
        out_shape=[
            jax.ShapeDtypeStruct((6, b, D_MODEL), _F32),
            jax.ShapeDtypeStruct((1, D_MODEL), _F32),
        ],
        name="prep",
    )(c, w_ada, b_ada.reshape(6, 1, D_MODEL), lb_logits)


def _pair_project(u, w_in_ref, p):
    off = p * HEAD_PAIR
    zq, zf, zi, zg = (_dot(u, _weight(w_in_ref[:, col + off:col + off + HEAD_PAIR]))
                      for col in (COL_Q, COL_F, COL_I, COL_G))
    return zq, zf, zi.astype(_BF16), zg


def _pair_gates(zq, zf, lb, tri):
    f = lb + (1.0 - lb) * _sigmoid(zf)
    l2f = jnp.log2(f)
    hi = l2f.astype(_BF16)
    lo = (l2f - hi.astype(_F32)).astype(_BF16)
    cum = _dot(tri, hi) + _dot(tri, lo)
    return _silu(zq), 1.0 - f, cum


def _chunk_head_scores(q, k, cum, v_b):
    cl = cum[CHUNK - 1:CHUNK]
    qd = (q * jnp.exp2(cum)).astype(_BF16)
    kd = (k * jnp.exp2(cl - cum)).astype(_BF16)
    upd = _dot_tn(kd, v_b)
    dec = jnp.transpose(jnp.broadcast_to(jnp.exp2(cl), (HG_DK, HG_DK)))

    def keys(ref, hi):
        return (k[0:hi] * jnp.exp2(-cum[0:hi] if ref is None else ref - cum[0:hi])).astype(_BF16)

    def queries(ref, lo):
        return (q[lo:lo + SUB_CHUNK] * jnp.exp2(cum[lo:lo + SUB_CHUNK] - ref)).astype(_BF16)

    zeros = jnp.zeros((SUB_CHUNK, HG_DK), _BF16)
    scores = []
    for i0 in range(0, N_SUB, 2):
        lo0, lo1 = i0 * SUB_CHUNK, (i0 + 1) * SUB_CHUNK
        hi = lo1 + SUB_CHUNK
        ref0 = None if i0 == 0 else cum[lo0 - 1:lo0]
        ref1 = cum[lo1 - 1:lo1]
        q0 = qd[lo0:lo1] if i0 == 0 else queries(ref0, lo0)
        q1 = queries(ref1, lo1)
        lhs = jnp.concatenate([jnp.concatenate([q0, zeros], axis=1),
                               jnp.concatenate([zeros, q1], axis=1)], axis=0)
        rhs = jnp.concatenate([jnp.concatenate([keys(ref0, lo1), zeros], axis=0),
                               keys(ref1, hi)], axis=1)
        scores.append(_dot_nt(lhs, rhs))
    return qd, v_b, dec, upd, scores


def _chunk_head_outputs(st, qd, v_b, scores):
    o_inter = _dot(qd, st.astype(_BF16))
    outs = []
    for j, sc in enumerate(scores):
        lo = 2 * j * SUB_CHUNK
        row = lax.broadcasted_iota(jnp.int32, sc.shape, 0) + lo
        col = lax.broadcasted_iota(jnp.int32, sc.shape, 1)
        sc = jnp.where(col <= row, sc, 0.0).astype(_BF16)
        outs.append(_dot(sc, v_b[0:sc.shape[1]]))
    return o_inter + jnp.concatenate(outs, axis=0)


def _block_kernel(n_seq_tiles,
                  xa_ref, xb_ref, moda_ref, modb_ref, modf_ref, lb_ref, w_in_ref, tri_ref, bgate_ref,
                  normw_ref, wpa_ref, lnw_ref, lnb_ref, ws_ref, bs_ref, wpb_ref, wout_ref, ln1w_ref,
                  ln1b_ref, w1_ref, w2_ref, ln2w_ref, ln2b_ref,
                  out_ref, st_ref, hg_ref, gm_ref, h1_ref, u2_ref, act_ref):
    s = pl.program_id(0)

    @pl.when(s == 0)
    def _():
        st_ref[...] = jnp.zeros_like(st_ref)
        h1_ref[...] = jnp.zeros_like(h1_ref)
        u2_ref[...] = jnp.zeros_like(u2_ref)

    half = TM // 2
    tri = tri_ref[...]
    norm_w = normw_ref[...]

    def chunk_head_slices(c, hh):
        return (slice(c * CHUNK, (c + 1) * CHUNK), slice(hh * HG_DK, (hh + 1) * HG_DK))

    def tile_step(tile, x_ref, mod_ref, modf_ref, slot, out_idx):
        prev = 1 - slot
        keep = jnp.where(tile % n_seq_tiles == 0, 0.0, 1.0).astype(_F32)

        def ffn_slice(i):
            u2 = u2_ref[prev]
            a = _dot(u2, _weight(w1_ref[:, i * MXU_N:(i + 1) * MXU_N]))
            b = _dot(u2, _weight(w1_ref[:, D_FF + i * MXU_N:D_FF + (i + 1) * MXU_N]))
            act_ref[:, i * MXU_N:(i + 1) * MXU_N] = (_silu(a) * b).astype(_BF16)

        def ffn_down(rows):
            return _dot(act_ref[rows, :], _weight(w2_ref[...]))

        def ffn_finish(rows, ffn):
            g2 = modf_ref[0, 5:6]
            r = DEEPNORM_ALPHA * h1_ref[prev, rows, :] + g2 * ffn
            out_ref[out_idx, rows, :] = _layer_norm(r) * ln2w_ref[...] + ln2b_ref[...]

        sh1, sc1, g1 = mod_ref[0, 0:1], mod_ref[0, 1:2], mod_ref[0, 2:3]
        for i in range(FFN_LEAD):
            ffn_slice(i)
        u = (_layer_norm(x_ref[0]) * (1.0 + sc1) + sh1).astype(_BF16)

        def stage2(z, gates):
            q, k, cum = gates
            return [[_chunk_head_scores(q[sl], k[sl], cum[sl], z[2][sl])
                     for sl in (chunk_head_slices(c, hh) for hh in range(2))]
                    for c in range(N_CHUNKS)]

        def stage3(p, s2):
            outs = []
            for hh in range(2):
                st = st_ref[2 * p + hh] * keep
                col = []
                for c in range(N_CHUNKS):
                    qd, v_b, dec, upd, scores = s2[c][hh]
                    col.append(_chunk_head_outputs(st, qd, v_b, scores))
                    st = st * dec + upd
                st_ref[2 * p + hh] = st
                outs.append(col)
            return outs

        def stage4(p, z, s3):
            for hh in range(2):
                hd = 2 * p + hh
                for c in range(N_CHUNKS):
                    sl = chunk_head_slices(c, hh)
                    o = s3[hh][c]
                    o = o * lax.rsqrt(jnp.mean(o * o, axis=-1, keepdims=True) + RMS_EPS) * norm_w
                    hg_ref[sl[0], hd * HG_DK:(hd + 1) * HG_DK] = (
                        o * _silu(z[3][sl])).astype(_BF16)

        field = {}
        z, s1, s2 = {}, {}, {}
        fillers = (COL_V, COL_U, COL_GA, COL_GB)
        next_slice = FFN_LEAD
        for r in range(N_PAIRS + len(fillers)):
            if r < N_PAIRS:
                z[r] = _pair_project(u, w_in_ref, r)
            else:
                col = fillers[r - N_PAIRS]
                field[col] = _dot(u, _weight(w_in_ref[:, col:col + D_MODEL]))
            for _ in range(FFN_PER_ROW):
                if next_slice < FF_SLICES:
                    ffn_slice(next_slice)
                    next_slice += 1
            p = r - 3
            if 0 <= p < N_PAIRS:
                stage4(p, z.pop(p), stage3(p, s2.pop(p)))
            p = r - 2
            if 0 <= p < N_PAIRS:
                s2[p] = stage2(z[p], s1.pop(p))
            p = r - 1
            if 0 <= p < N_PAIRS:
                lb = lb_ref[:, p * HEAD_PAIR:(p + 1) * HEAD_PAIR]
                s1[p] = _pair_gates(z[p][0], z[p][1], lb, tri)
        while next_slice < FF_SLICES:
            ffn_slice(next_slice)
            next_slice += 1

        gv = (_layer_norm(_gelu(field[COL_V])) * lnw_ref[...] + lnb_ref[...]).astype(_BF16)
        gu = _gelu(field[COL_U])
        n_blocks = TM // GM_BLOCK
        for g in range(GM_GROUPS):
            cols = slice(g * GM_CG, (g + 1) * GM_CG)
            rhs = jnp.concatenate([gv[nb * GM_BLOCK:(nb + 1) * GM_BLOCK, cols]
                                   for nb in range(n_blocks)], axis=1)
            sv = _dot(ws_ref[g], rhs)
            for nb in range(n_blocks):
                rows = slice(nb * GM_BLOCK, (nb + 1) * GM_BLOCK)
                sv_nb = sv[:, nb * GM_CG:(nb + 1) * GM_CG] + bs_ref[g]
                gm_ref[rows, cols] = (gu[rows, cols] * sv_nb).astype(_BF16)

        top, bot = slice(0, half), slice(half, TM)
        y_b = _dot(gm_ref[...], _weight(wpb_ref[...]))
        y_a = _dot(hg_ref[...], _weight(wpa_ref[...]))
        mix = (_sigmoid(field[COL_GA] + bgate_ref[0:1]) * y_a
               + _sigmoid(field[COL_GB] + bgate_ref[1:2]) * y_b).astype(_BF16)
        m = _dot(mix, _weight(wout_ref[...]))
        ffn_top = ffn_down(top)
        ffn_bot = ffn_down(bot)
        h1 = _layer_norm(DEEPNORM_ALPHA * x_ref[0] + g1 * m) * ln1w_ref[...] + ln1b_ref[...]
        h1_ref[slot] = h1
        sh2, sc2 = mod_ref[0, 3:4], mod_ref[0, 4:5]
        u2_ref[slot] = (_layer_norm(h1) * (1.0 + sc2) + sh2).astype(_BF16)
        ffn_finish(top, ffn_top)
        ffn_finish(bot, ffn_bot)

    tile_step(2 * s - 1, xa_ref, moda_ref, modf_ref, 0, 0)
    tile_step(2 * s, xb_ref, modb_ref, moda_ref, 1, 1)


def _block(x, mod, consts):
    n_tiles, _, d = x.shape
    n_batch = mod.shape[0]
    n_seq_tiles = n_tiles // n_batch
    last = n_tiles - 1
    assert n_tiles % 2 == 0 and n_seq_tiles % 2 == 0

    def const_spec(a):
        return pl.BlockSpec(a.shape, lambda s: (0,) * a.ndim, pipeline_mode=pl.Buffered(1))

    def tile_spec(offset):
        return pl.BlockSpec((1, TM, d), lambda s: (jnp.clip(2 * s + offset, 0, last), 0, 0))

    def mod_spec(offset):
        return pl.BlockSpec(
            (1, 6, d), lambda s: (jnp.clip(2 * s + offset, 0, last) // n_seq_tiles, 0, 0))

    return pl.pallas_call(
        functools.partial(_block_kernel, n_seq_tiles),
        grid=(n_tiles // 2 + 1,),
        in_specs=[tile_spec(-1), tile_spec(0), mod_spec(-1), mod_spec(0), mod_spec(-2)]
        + [const_spec(a) for a in consts],
        out_specs=pl.BlockSpec((2, TM, d), lambda s: (jnp.maximum(s - 1, 0), 0, 0)),
        out_shape=jax.ShapeDtypeStruct((n_tiles, TM, d), _F32),
        scratch_shapes=[
            pltpu.VMEM((HG_HEADS, HG_DK, HG_DK), _F32),
            pltpu.VMEM((TM, d), _BF16),
            pltpu.VMEM((TM, d), _BF16),
            pltpu.VMEM((2, TM, d), _F32),
            pltpu.VMEM((2, TM, d), _BF16),
            pltpu.VMEM((TM, D_FF), _BF16),
        ],
        compiler_params=pltpu.CompilerParams(
            dimension_semantics=("arbitrary",),
            vmem_limit_bytes=VMEM_LIMIT),
        name="block",
    )(x, x, mod, mod, mod, *consts)


def _chunk_tri(n):
    r = jnp.arange(n)[:, None]
    c = jnp.arange(n)[None, :]
    return ((r >= c) & (r // CHUNK == c // CHUNK)).astype(_BF16)


def kernel(x, c, w_ada, b_ada, w_in, b_gate, hgrn_lb_logits, hgrn_norm_w, w_proj_a, gmlp_ln_w,
           gmlp_ln_b, gmlp_ws, gmlp_bs, w_proj_b, w_out, ln1_w, ln1_b, w_ffn_in, w_ffn_out,
           ln2_w, ln2_b):
    assert w_ada.shape[0] == 1, "single-layer block"
    b, s, d = x.shape
    row = lambda a: a.reshape(1, -1)
    mod, lb = _prep(c, w_ada[0], b_ada[0], hgrn_lb_logits)
    mod = mod.transpose(1, 0, 2)

    pos = jnp.arange(GM_BLOCK) // CHUNK
    ws = jnp.where(pos[:, None] >= pos[None, :], gmlp_ws[0], 0.0).astype(_BF16)
    bs = jnp.broadcast_to(gmlp_bs[0][:, :, None], (GM_GROUPS, GM_BLOCK, GM_CG))

    p_in, p_a, p_b, p_out, p_ffn_in, p_ffn_out = _pack_weights(
        w_in[0], w_proj_a[0], w_proj_b[0], w_out[0], w_ffn_in[0], w_ffn_out[0])
    consts = (lb, p_in, _chunk_tri(TM), b_gate[0], row(hgrn_norm_w[0]),
              p_a, row(gmlp_ln_w[0]), row(gmlp_ln_b[0]), ws, bs,
              p_b, p_out, row(ln1_w[0]), row(ln1_b[0]),
              p_ffn_in, p_ffn_out, row(ln2_w[0]), row(ln2_b[0]))
    out = _block(x.reshape(b * s // TM, TM, d), mod, consts)
    return out.reshape(b, s, d)
```

```python
import functools

import jax
import jax.numpy as jnp
from jax import lax
from jax.experimental import pallas as pl
from jax.experimental.pallas import tpu as pltpu

D_MODEL = 1024
CHUNK = 64
SUB_CHUNK = 16
N_SUB = CHUNK // SUB_CHUNK
HG_DK = 128
HG_HEADS = D_MODEL // HG_DK
GM_BLOCK = 128
GM_GROUPS = 8
GM_CG = D_MODEL // GM_GROUPS
D_FF = 2816
DEEPNORM_ALPHA = 2.0 ** 0.25
LN_EPS = 1e-5
RMS_EPS = 1e-6

COL_Q, COL_F, COL_I, COL_G, COL_U, COL_V, COL_GA, COL_GB = (i * D_MODEL for i in range(8))

TM = 256
MXU_N = 256
HEAD_PAIR = MXU_N
N_CHUNKS = TM // CHUNK
N_PAIRS = HG_HEADS // 2
FF_SLICES = D_FF // MXU_N
FFN_LEAD = 4
FFN_PER_ROW = 1
PACK_STEPS = 8
VMEM_LIMIT = 63 * 1024 * 1024 + 512 * 1024

_F32 = jnp.float32
_BF16 = jnp.bfloat16


def _dot(a, b):
    return jnp.dot(a, b, preferred_element_type=_F32)


def _dot_nt(a, b):
    return lax.dot_general(a, b, (((1,), (1,)), ((), ())), preferred_element_type=_F32)


def _dot_tn(a, b):
    return lax.dot_general(a, b, (((0,), (0,)), ((), ())), preferred_element_type=_F32)


def _pack_kernel(*refs):
    n = len(refs) // 2
    for w_ref, o_ref in zip(refs[:n], refs[n:]):
        o_ref[...] = pltpu.bitcast(w_ref[...].astype(_BF16), jnp.uint32)


def _pack_weights(*ws):
    for w in ws:
        assert w.shape[0] % (16 * PACK_STEPS) == 0, w.shape
    rows = [w.shape[0] // PACK_STEPS for w in ws]
    return pl.pallas_call(
        _pack_kernel,
        grid=(PACK_STEPS,),
        in_specs=[pl.BlockSpec((r, w.shape[1]), lambda i: (i, 0)) for r, w in zip(rows, ws)],
        out_specs=[pl.BlockSpec((r // 2, w.shape[1]), lambda i: (i, 0)) for r, w in zip(rows, ws)],
        out_shape=[jax.ShapeDtypeStruct((w.shape[0] // 2, w.shape[1]), jnp.uint32) for w in ws],
        compiler_params=pltpu.CompilerParams(vmem_limit_bytes=VMEM_LIMIT),
        name="pack",
    )(*ws)


def _weight(packed):
    return pltpu.bitcast(packed, _BF16)


def _layer_norm(x):
    xc = x - jnp.mean(x, axis=-1, keepdims=True)
    return xc * lax.rsqrt(jnp.mean(xc * xc, axis=-1, keepdims=True) + LN_EPS)


def _sigmoid(x):
    return 1.0 / (1.0 + jnp.exp(-x))


def _silu(x):
    return x * _sigmoid(x)


def _gelu(x):
    return 0.5 * x * (1.0 + lax.erf(x * (2.0 ** -0.5)))


def _prep_kernel(c_ref, w_ref, b_ref, lbl_ref, mod_ref, lb_ref):
    cond = _silu(c_ref[...])
    mod_ref[0] = _dot(cond.astype(_BF16), w_ref[...].astype(_BF16)) + b_ref[0]
    logits = lbl_ref[...]
    e = jnp.exp(logits - jnp.max(logits, axis=0, keepdims=True))
    lb_ref[...] = e[0:1] / jnp.sum(e, axis=0, keepdims=True)


def _prep(c, w_ada, b_ada, lb_logits):
    b = c.shape[0]
    return pl.pallas_call(
        _prep_kernel,
        grid=(6,),
        in_specs=[
            pl.BlockSpec((b, D_MODEL), lambda j: (0, 0)),
            pl.BlockSpec((D_MODEL, D_MODEL), lambda j: (0, j)),
            pl.BlockSpec((1, 1, D_MODEL), lambda j: (j, 0, 0)),
            pl.BlockSpec(lb_logits.shape, lambda j: (0, 0)),
        ],
        out_specs=[
            pl.BlockSpec((1, b, D_MODEL), lambda j: (j, 0, 0)),
            pl.BlockSpec((1, D_MODEL), lambda j: (0, 0)),
        ],
        out_shape=[
            jax.ShapeDtypeStruct((6, b, D_MODEL), _F32),
            jax.ShapeDtypeStruct((1, D_MODEL), _F32),
        ],
        name="prep",
    )(c, w_ada, b_ada.reshape(6, 1, D_MODEL), lb_logits)


def _pair_project(u, w_in_ref, p):
    off = p * HEAD_PAIR
    zq, zf, zi, zg = (_dot(u, _weight(w_in_ref[:, col + off:col + off + HEAD_PAIR]))
                      for col in (COL_Q, COL_F, COL_I, COL_G))
    return zq, zf, zi.astype(_BF16), zg


def _pair_gates(zq, zf, lb, tri):
    f = lb + (1.0 - lb) * _sigmoid(zf)
    l2f = jnp.log2(f)
    hi = l2f.astype(_BF16)
    lo = (l2f - hi.astype(_F32)).astype(_BF16)
    cum = _dot(tri, hi) + _dot(tri, lo)
    return _silu(zq), 1.0 - f, cum


def _chunk_head_operands(q, k, cum, v_b):
    cl = cum[CHUNK - 1:CHUNK]
    qd = (q * jnp.exp2(cum)).astype(_BF16)
    kd = (k * jnp.exp2(cl - cum)).astype(_BF16)
    dec = jnp.transpose(jnp.broadcast_to(jnp.exp2(cl), (HG_DK, HG_DK)))

    def keys(ref, hi):
        return (k[0:hi] * jnp.exp2(-cum[0:hi] if ref is None else ref - cum[0:hi])).astype(_BF16)

    def queries(ref, lo):
        return (q[lo:lo + SUB_CHUNK] * jnp.exp2(cum[lo:lo + SUB_CHUNK] - ref)).astype(_BF16)

    zeros = jnp.zeros((SUB_CHUNK, HG_DK), _BF16)
    scores = []
    for i0 in range(0, N_SUB, 2):
        lo0, lo1 = i0 * SUB_CHUNK, (i0 + 1) * SUB_CHUNK
        hi = lo1 + SUB_CHUNK
        ref0 = None if i0 == 0 else cum[lo0 - 1:lo0]
        ref1 = cum[lo1 - 1:lo1]
        q0 = qd[lo0:lo1] if i0 == 0 else queries(ref0, lo0)
        q1 = queries(ref1, lo1)
        lhs = jnp.concatenate([jnp.concatenate([q0, zeros], axis=1),
                               jnp.concatenate([zeros, q1], axis=1)], axis=0)
        rhs = jnp.concatenate([jnp.concatenate([keys(ref0, lo1), zeros], axis=0),
                               keys(ref1, hi)], axis=1)
        scores.append((lhs, rhs))
    return qd, kd, v_b, dec, scores


def _chunk_head_score_dots(operands):
    qd, kd, v_b, dec, pairs = operands
    upd = _dot_tn(kd, v_b)
    return qd, v_b, dec, upd, [_dot_nt(lhs, rhs) for lhs, rhs in pairs]


def _chunk_head_outputs(st, qd, v_b, scores):
    o_inter = _dot(qd, st.astype(_BF16))
    outs = []
    for j, sc in enumerate(scores):
        lo = 2 * j * SUB_CHUNK
        row = lax.broadcasted_iota(jnp.int32, sc.shape, 0) + lo
        col = lax.broadcasted_iota(jnp.int32, sc.shape, 1)
        sc = jnp.where(col <= row, sc, 0.0).astype(_BF16)
        outs.append(_dot(sc, v_b[0:sc.shape[1]]))
    return o_inter + jnp.concatenate(outs, axis=0)


def _block_kernel(n_seq_tiles,
                  xa_ref, xb_ref, moda_ref, modb_ref, modf_ref, lb_ref, w_in_ref, tri_ref, bgate_ref,
                  normw_ref, wpa_ref, lnw_ref, lnb_ref, ws_ref, bs_ref, wpb_ref, wout_ref, ln1w_ref,
                  ln1b_ref, w1_ref, w2_ref, ln2w_ref, ln2b_ref,
                  out_ref, st_ref, hg_ref, gm_ref, h1_ref, u2_ref, act_ref):
    s = pl.program_id(0)

    @pl.when(s == 0)
    def _():
        st_ref[...] = jnp.zeros_like(st_ref)
        h1_ref[...] = jnp.zeros_like(h1_ref)
        u2_ref[...] = jnp.zeros_like(u2_ref)

    half = TM // 2
    tri = tri_ref[...]
    norm_w = normw_ref[...]

    def chunk_head_slices(c, hh):
        return (slice(c * CHUNK, (c + 1) * CHUNK), slice(hh * HG_DK, (hh + 1) * HG_DK))

    def tile_step(tile, x_ref, mod_ref, modf_ref, slot, out_idx):
        prev = 1 - slot
        keep = jnp.where(tile % n_seq_tiles == 0, 0.0, 1.0).astype(_F32)

        def ffn_slice(i):
            u2 = u2_ref[prev]
            a = _dot(u2, _weight(w1_ref[:, i * MXU_N:(i + 1) * MXU_N]))
            b = _dot(u2, _weight(w1_ref[:, D_FF + i * MXU_N:D_FF + (i + 1) * MXU_N]))
            act_ref[:, i * MXU_N:(i + 1) * MXU_N] = (_silu(a) * b).astype(_BF16)

        def ffn_down(rows):
            return _dot(act_ref[rows, :], _weight(w2_ref[...]))

        def ffn_finish(rows, ffn):
            g2 = modf_ref[0, 5:6]
            r = DEEPNORM_ALPHA * h1_ref[prev, rows, :] + g2 * ffn
            out_ref[out_idx, rows, :] = _layer_norm(r) * ln2w_ref[...] + ln2b_ref[...]

        sh1, sc1, g1 = mod_ref[0, 0:1], mod_ref[0, 1:2], mod_ref[0, 2:3]
        for i in range(FFN_LEAD):
            ffn_slice(i)
        u = (_layer_norm(x_ref[0]) * (1.0 + sc1) + sh1).astype(_BF16)

        def stage2(z, gates):
            q, k, cum = gates
            return [[_chunk_head_operands(q[sl], k[sl], cum[sl], z[2][sl])
                     for sl in (chunk_head_slices(c, hh) for hh in range(2))]
                    for c in range(N_CHUNKS)]

        def stage3(p, s2):
            outs = []
            for hh in range(2):
                st = st_ref[2 * p + hh] * keep
                col = []
                for c in range(N_CHUNKS):
                    qd, v_b, dec, upd, scores = s2[c][hh]
                    col.append(_chunk_head_outputs(st, qd, v_b, scores))
                    st = st * dec + upd
                st_ref[2 * p + hh] = st
                outs.append(col)
            return outs

        def stage4(p, z, s3):
            for hh in range(2):
                hd = 2 * p + hh
                for c in range(N_CHUNKS):
                    sl = chunk_head_slices(c, hh)
                    o = s3[hh][c]
                    o = o * lax.rsqrt(jnp.mean(o * o, axis=-1, keepdims=True) + RMS_EPS) * norm_w
                    hg_ref[sl[0], hd * HG_DK:(hd + 1) * HG_DK] = (
                        o * _silu(z[3][sl])).astype(_BF16)

        field = {}
        z, s1, s2, operands = {}, {}, {}, {}
        fillers = (COL_V, COL_U, COL_GA, COL_GB)
        next_slice = FFN_LEAD
        for r in range(N_PAIRS + len(fillers)):
            p = r - 2
            if 0 <= p < N_PAIRS:
                operands[p] = stage2(z[p], s1.pop(p))
            if r < N_PAIRS:
                z[r] = _pair_project(u, w_in_ref, r)
            else:
                col = fillers[r - N_PAIRS]
                field[col] = _dot(u, _weight(w_in_ref[:, col:col + D_MODEL]))
            for _ in range(FFN_PER_ROW):
                if next_slice < FF_SLICES:
                    ffn_slice(next_slice)
                    next_slice += 1
            p = r - 3
            if 0 <= p < N_PAIRS:
                stage4(p, z.pop(p), stage3(p, s2.pop(p)))
            p = r - 2
            if 0 <= p < N_PAIRS:
                s2[p] = [[_chunk_head_score_dots(o) for o in row] for row in operands.pop(p)]
            p = r - 1
            if 0 <= p < N_PAIRS:
                lb = lb_ref[:, p * HEAD_PAIR:(p + 1) * HEAD_PAIR]
                s1[p] = _pair_gates(z[p][0], z[p][1], lb, tri)
        while next_slice < FF_SLICES:
            ffn_slice(next_slice)
            next_slice += 1

        gv = (_layer_norm(_gelu(field[COL_V])) * lnw_ref[...] + lnb_ref[...]).astype(_BF16)
        gu = _gelu(field[COL_U])
        n_blocks = TM // GM_BLOCK
        for g in range(GM_GROUPS):
            cols = slice(g * GM_CG, (g + 1) * GM_CG)
            rhs = jnp.concatenate([gv[nb * GM_BLOCK:(nb + 1) * GM_BLOCK, cols]
                                   for nb in range(n_blocks)], axis=1)
            sv = _dot(ws_ref[g], rhs)
            for nb in range(n_blocks):
                rows = slice(nb * GM_BLOCK, (nb + 1) * GM_BLOCK)
                sv_nb = sv[:, nb * GM_CG:(nb + 1) * GM_CG] + bs_ref[g]
                gm_ref[rows, cols] = (gu[rows, cols] * sv_nb).astype(_BF16)

        top, bot = slice(0, half), slice(half, TM)
        y_b = _dot(gm_ref[...], _weight(wpb_ref[...]))
        y_a = _dot(hg_ref[...], _weight(wpa_ref[...]))
        mix = (_sigmoid(field[COL_GA] + bgate_ref[0:1]) * y_a
               + _sigmoid(field[COL_GB] + bgate_ref[1:2]) * y_b).astype(_BF16)
        m = _dot(mix, _weight(wout_ref[...]))
        ffn_top = ffn_down(top)
        ffn_bot = ffn_down(bot)
        h1 = _layer_norm(DEEPNORM_ALPHA * x_ref[0] + g1 * m) * ln1w_ref[...] + ln1b_ref[...]
        h1_ref[slot] = h1
        sh2, sc2 = mod_ref[0, 3:4], mod_ref[0, 4:5]
        u2_ref[slot] = (_layer_norm(h1) * (1.0 + sc2) + sh2).astype(_BF16)
        ffn_finish(top, ffn_top)
        ffn_finish(bot, ffn_bot)

    tile_step(2 * s - 1, xa_ref, moda_ref, modf_ref, 0, 0)
    tile_step(2 * s, xb_ref, modb_ref, moda_ref, 1, 1)


def _block(x, mod, consts):
    n_tiles, _, d = x.shape
    n_batch = mod.shape[0]
    n_seq_tiles = n_tiles // n_batch
    last = n_tiles - 1
    assert n_tiles % 2 == 0 and n_seq_tiles % 2 == 0

    def const_spec(a):
        return pl.BlockSpec(a.shape, lambda s: (0,) * a.ndim, pipeline_mode=pl.Buffered(1))

    def tile_spec(offset):
        return pl.BlockSpec((1, TM, d), lambda s: (jnp.clip(2 * s + offset, 0, last), 0, 0))

    def mod_spec(offset):
        return pl.BlockSpec(
            (1, 6, d), lambda s: (jnp.clip(2 * s + offset, 0, last) // n_seq_tiles, 0, 0))

    return pl.pallas_call(
        functools.partial(_block_kernel, n_seq_tiles),
        grid=(n_tiles // 2 + 1,),
        in_specs=[tile_spec(-1), tile_spec(0), mod_spec(-1), mod_spec(0), mod_spec(-2)]
        + [const_spec(a) for a in consts],
        out_specs=pl.BlockSpec((2, TM, d), lambda s: (jnp.maximum(s - 1, 0), 0, 0)),
        out_shape=jax.ShapeDtypeStruct((n_tiles, TM, d), _F32),
        scratch_shapes=[
            pltpu.VMEM((HG_HEADS, HG_DK, HG_DK), _F32),
            pltpu.VMEM((TM, d), _BF16),
            pltpu.VMEM((TM, d), _BF16),
            pltpu.VMEM((2, TM, d), _F32),
            pltpu.VMEM((2, TM, d), _BF16),
            pltpu.VMEM((TM, D_FF), _BF16),
        ],
        compiler_params=pltpu.CompilerParams(
            dimension_semantics=("arbitrary",),
            vmem_limit_bytes=VMEM_LIMIT),
        name="block",
    )(x, x, mod, mod, mod, *consts)


def _chunk_tri(n):
    r = jnp.arange(n)[:, None]
    c = jnp.arange(n)[None, :]
    return ((r >= c) & (r // CHUNK == c // CHUNK)).astype(_BF16)


def kernel(x, c, w_ada, b_ada, w_in, b_gate, hgrn_lb_logits, hgrn_norm_w, w_proj_a, gmlp_ln_w,
           gmlp_ln_b, gmlp_ws, gmlp_bs, w_proj_b, w_out, ln1_w, ln1_b, w_ffn_in, w_ffn_out,
           ln2_w, ln2_b):
    assert w_ada.shape[0] == 1, "single-layer block"
    b, s, d = x.shape
    row = lambda a: a.reshape(1, -1)
    mod, lb = _prep(c, w_ada[0], b_ada[0], hgrn_lb_logits)
    mod = mod.transpose(1, 0, 2)

    pos = jnp.arange(GM_BLOCK) // CHUNK
    ws = jnp.where(pos[:, None] >= pos[None, :], gmlp_ws[0], 0.0).astype(_BF16)
    bs = jnp.broadcast_to(gmlp_bs[0][:, :, None], (GM_GROUPS, GM_BLOCK, GM_CG))

    p_in, p_a, p_b, p_out, p_ffn_in, p_ffn_out = _pack_weights(
        w_in[0], w_proj_a[0], w_proj_b[0], w_out[0], w_ffn_in[0], w_ffn_out[0])
    consts = (lb, p_in, _chunk_tri(TM), b_gate[0], row(hgrn_norm_w[0]),
              p_a, row(gmlp_ln_w[0]), row(gmlp_ln_b[0]), ws, bs,
              p_b, p_out, row(ln1_w[0]), row(ln1_b[0]),
              p_ffn_in, p_ffn_out, row(ln2_w[0]), row(ln2_b[0]))
    out = _block(x.reshape(b * s // TM, TM, d), mod, consts)
    return out.reshape(b, s, d)
```

```python
import functools

import jax
import jax.numpy as jnp
from jax import lax
from jax.experimental import pallas as pl
from jax.experimental.pallas import tpu as pltpu

D_MODEL = 1024
CHUNK = 64
SUB_CHUNK = 16
N_SUB = CHUNK // SUB_CHUNK
HG_DK = 128
HG_HEADS = D_MODEL // HG_DK
GM_BLOCK = 128
GM_GROUPS = 8
GM_CG = D_MODEL // GM_GROUPS
D_FF = 2816
DEEPNORM_ALPHA = 2.0 ** 0.25
LN_EPS = 1e-5
RMS_EPS = 1e-6

COL_Q, COL_F, COL_I, COL_G, COL_U, COL_V, COL_GA, COL_GB = (i * D_MODEL for i in range(8))

TM = 256
MXU_N = 256
HEAD_PAIR = MXU_N
N_CHUNKS = TM // CHUNK
N_PAIRS = HG_HEADS // 2
FF_SLICES = D_FF // MXU_N
FFN_LEAD = 4
FFN_PER_ROW = 1
PACK_STEPS = 8
VMEM_LIMIT = 63 * 1024 * 1024 + 512 * 1024

_F32 = jnp.float32
_BF16 = jnp.bfloat16


def _dot(a, b):
    return jnp.dot(a, b, preferred_element_type=_F32)


def _dot_nt(a, b):
    return lax.dot_general(a, b, (((1,), (1,)), ((), ())), preferred_element_type=_F32)


def _dot_tn(a, b):
    return lax.dot_general(a, b, (((0,), (0,)), ((), ())), preferred_element_type=_F32)


def _pack_kernel(*refs):
    n = len(refs) // 2
    for w_ref, o_ref in zip(refs[:n], refs[n:]):
        o_ref[...] = pltpu.bitcast(w_ref[...].astype(_BF16), jnp.uint32)


def _pack_weights(*ws):
    for w in ws:
        assert w.shape[0] % (16 * PACK_STEPS) == 0, w.shape
    rows = [w.shape[0] // PACK_STEPS for w in ws]
    return pl.pallas_call(
        _pack_kernel,
        grid=(PACK_STEPS,),
        in_specs=[pl.BlockSpec((r, w.shape[1]), lambda i: (i, 0)) for r, w in zip(rows, ws)],
        out_specs=[pl.BlockSpec((r // 2, w.shape[1]), lambda i: (i, 0)) for r, w in zip(rows, ws)],
        out_shape=[jax.ShapeDtypeStruct((w.shape[0] // 2, w.shape[1]), jnp.uint32) for w in ws],
        compiler_params=pltpu.CompilerParams(vmem_limit_bytes=VMEM_LIMIT),
        name="pack",
    )(*ws)


def _weight(packed):
    return pltpu.bitcast(packed, _BF16)


def _layer_norm(x):
    xc = x - jnp.mean(x, axis=-1, keepdims=True)
    return xc * lax.rsqrt(jnp.mean(xc * xc, axis=-1, keepdims=True) + LN_EPS)


def _sigmoid(x):
    return 1.0 / (1.0 + jnp.exp(-x))


def _silu(x):
    return x * _sigmoid(x)


def _gelu(x):
    return 0.5 * x * (1.0 + lax.erf(x * (2.0 ** -0.5)))


def _prep_kernel(c_ref, w_ref, b_ref, lbl_ref, mod_ref, lb_ref):
    cond = _silu(c_ref[...])
    mod_ref[0] = _dot(cond.astype(_BF16), w_ref[...].astype(_BF16)) + b_ref[0]
    logits = lbl_ref[...]
    e = jnp.exp(logits - jnp.max(logits, axis=0, keepdims=True))
    lb_ref[...] = e[0:1] / jnp.sum(e, axis=0, keepdims=True)


def _prep(c, w_ada, b_ada, lb_logits):
    b = c.shape[0]
    return pl.pallas_call(
        _prep_kernel,
        grid=(6,),
        in_specs=[
            pl.BlockSpec((b, D_MODEL), lambda j: (0, 0)),
            pl.BlockSpec((D_MODEL, D_MODEL), lambda j: (0, j)),
            pl.BlockSpec((1, 1, D_MODEL), lambda j: (j, 0, 0)),
            pl.BlockSpec(lb_logits.shape, lambda j: (0, 0)),
        ],
        out_specs=[
            pl.BlockSpec((1, b, D_MODEL), lambda j: (j, 0, 0)),
            pl.BlockSpec((1, D_MODEL), lambda j: (0, 0)),
        ],
        out_shape=[
            jax.ShapeDtypeStruct((6, b, D_MODEL), _F32),
            jax.ShapeDtypeStruct((1, D_MODEL), _F32),
        ],
        name="prep",
    )(c, w_ada, b_ada.reshape(6, 1, D_MODEL), lb_logits)


def _pair_project(u, w_in_ref, p):
    off = p * HEAD_PAIR
    zq, zf, zi, zg = (_dot(u, _weight(w_in_ref[:, col + off:col + off + HEAD_PAIR]))
                      for col in (COL_Q, COL_F, COL_I, COL_G))
    return zq, zf, zi.astype(_BF16), zg


def _pair_gates(zq, zf, lb):
    f = lb + (1.0 - lb) * _sigmoid(zf)
    l2f = jnp.log2(f)
    hi = l2f.astype(_BF16)
    lo = (l2f - hi.astype(_F32)).astype(_BF16)
    return _silu(zq), 1.0 - f, hi, lo


def _pair_cumsum(gates, tri):
    q, k, hi, lo = gates
    return q, k, _dot(tri, hi) + _dot(tri, lo)


def _chunk_head_operands(q, k, cum, v_b):
    cl = cum[CHUNK - 1:CHUNK]
    qd = (q * jnp.exp2(cum)).astype(_BF16)
    kd = (k * jnp.exp2(cl - cum)).astype(_BF16)
    dec = jnp.transpose(jnp.broadcast_to(jnp.exp2(cl), (HG_DK, HG_DK)))

    def keys(ref, hi):
        return (k[0:hi] * jnp.exp2(-cum[0:hi] if ref is None else ref - cum[0:hi])).astype(_BF16)

    def queries(ref, lo):
        return (q[lo:lo + SUB_CHUNK] * jnp.exp2(cum[lo:lo + SUB_CHUNK] - ref)).astype(_BF16)

    zeros = jnp.zeros((SUB_CHUNK, HG_DK), _BF16)
    scores = []
    for i0 in range(0, N_SUB, 2):
        lo0, lo1 = i0 * SUB_CHUNK, (i0 + 1) * SUB_CHUNK
        hi = lo1 + SUB_CHUNK
        ref0 = None if i0 == 0 else cum[lo0 - 1:lo0]
        ref1 = cum[lo1 - 1:lo1]
        q0 = qd[lo0:lo1] if i0 == 0 else queries(ref0, lo0)
        q1 = queries(ref1, lo1)
        lhs = jnp.concatenate([jnp.concatenate([q0, zeros], axis=1),
                               jnp.concatenate([zeros, q1], axis=1)], axis=0)
        rhs = jnp.concatenate([jnp.concatenate([keys(ref0, lo1), zeros], axis=0),
                               keys(ref1, hi)], axis=1)
        scores.append((lhs, rhs))
    return qd, kd, v_b, dec, scores


def _chunk_head_score_dots(operands):
    qd, kd, v_b, dec, pairs = operands
    upd = _dot_tn(kd, v_b)
    return qd, v_b, dec, upd, [_dot_nt(lhs, rhs) for lhs, rhs in pairs]


def _chunk_head_outputs(st, qd, v_b, scores):
    o_inter = _dot(qd, st.astype(_BF16))
    outs = []
    for j, sc in enumerate(scores):
        lo = 2 * j * SUB_CHUNK
        row = lax.broadcasted_iota(jnp.int32, sc.shape, 0) + lo
        col = lax.broadcasted_iota(jnp.int32, sc.shape, 1)
        sc = jnp.where(col <= row, sc, 0.0).astype(_BF16)
        outs.append(_dot(sc, v_b[0:sc.shape[1]]))
    return o_inter + jnp.concatenate(outs, axis=0)


def _block_kernel(n_seq_tiles,
                  xa_ref, xb_ref, moda_ref, modb_ref, modf_ref, lb_ref, w_in_ref, tri_ref, bgate_ref,
                  normw_ref, wpa_ref, lnw_ref, lnb_ref, ws_ref, bs_ref, wpb_ref, wout_ref, ln1w_ref,
                  ln1b_ref, w1_ref, w2_ref, ln2w_ref, ln2b_ref,
                  out_ref, st_ref, hg_ref, gm_ref, h1_ref, u2_ref, act_ref):
    s = pl.program_id(0)

    @pl.when(s == 0)
    def _():
        st_ref[...] = jnp.zeros_like(st_ref)
        h1_ref[...] = jnp.zeros_like(h1_ref)
        u2_ref[...] = jnp.zeros_like(u2_ref)

    half = TM // 2
    tri = tri_ref[...]
    norm_w = normw_ref[...]

    def chunk_head_slices(c, hh):
        return (slice(c * CHUNK, (c + 1) * CHUNK), slice(hh * HG_DK, (hh + 1) * HG_DK))

    def tile_step(tile, x_ref, mod_ref, modf_ref, slot, out_idx):
        prev = 1 - slot
        keep = jnp.where(tile % n_seq_tiles == 0, 0.0, 1.0).astype(_F32)

        def ffn_slice(i):
            u2 = u2_ref[prev]
            a = _dot(u2, _weight(w1_ref[:, i * MXU_N:(i + 1) * MXU_N]))
            b = _dot(u2, _weight(w1_ref[:, D_FF + i * MXU_N:D_FF + (i + 1) * MXU_N]))
            act_ref[:, i * MXU_N:(i + 1) * MXU_N] = (_silu(a) * b).astype(_BF16)

        def ffn_down(rows):
            return _dot(act_ref[rows, :], _weight(w2_ref[...]))

        def ffn_finish(rows, ffn):
            g2 = modf_ref[0, 5:6]
            r = DEEPNORM_ALPHA * h1_ref[prev, rows, :] + g2 * ffn
            out_ref[out_idx, rows, :] = _layer_norm(r) * ln2w_ref[...] + ln2b_ref[...]

        sh1, sc1, g1 = mod_ref[0, 0:1], mod_ref[0, 1:2], mod_ref[0, 2:3]
        for i in range(FFN_LEAD):
            ffn_slice(i)
        u = (_layer_norm(x_ref[0]) * (1.0 + sc1) + sh1).astype(_BF16)

        def stage2(z, gates):
            q, k, cum = gates
            return [[_chunk_head_operands(q[sl], k[sl], cum[sl], z[2][sl])
                     for sl in (chunk_head_slices(c, hh) for hh in range(2))]
                    for c in range(N_CHUNKS)]

        def stage3(p, s2):
            outs = []
            for hh in range(2):
                st = st_ref[2 * p + hh] * keep
                col = []
                for c in range(N_CHUNKS):
                    qd, v_b, dec, upd, scores = s2[c][hh]
                    col.append(_chunk_head_outputs(st, qd, v_b, scores))
                    st = st * dec + upd
                st_ref[2 * p + hh] = st
                outs.append(col)
            return outs

        def stage4(p, z, s3):
            for hh in range(2):
                hd = 2 * p + hh
                for c in range(N_CHUNKS):
                    sl = chunk_head_slices(c, hh)
                    o = s3[hh][c]
                    o = o * lax.rsqrt(jnp.mean(o * o, axis=-1, keepdims=True) + RMS_EPS) * norm_w
                    hg_ref[sl[0], hd * HG_DK:(hd + 1) * HG_DK] = (
                        o * _silu(z[3][sl])).astype(_BF16)

        field = {}
        z, s1, s2, operands, gates = {}, {}, {}, {}, {}
        fillers = (COL_V, COL_U, COL_GA, COL_GB)
        next_slice = FFN_LEAD
        for r in range(N_PAIRS + len(fillers)):
            p = r - 2
            if 0 <= p < N_PAIRS:
                operands[p] = stage2(z[p], s1.pop(p))
            p = r - 1
            if 0 <= p < N_PAIRS:
                lb = lb_ref[:, p * HEAD_PAIR:(p + 1) * HEAD_PAIR]
                gates[p] = _pair_gates(z[p][0], z[p][1], lb)
            if r < N_PAIRS:
                z[r] = _pair_project(u, w_in_ref, r)
            else:
                col = fillers[r - N_PAIRS]
                field[col] = _dot(u, _weight(w_in_ref[:, col:col + D_MODEL]))
            for _ in range(FFN_PER_ROW):
                if next_slice < FF_SLICES:
                    ffn_slice(next_slice)
                    next_slice += 1
            p = r - 3
            if 0 <= p < N_PAIRS:
                stage4(p, z.pop(p), stage3(p, s2.pop(p)))
            p = r - 2
            if 0 <= p < N_PAIRS:
                s2[p] = [[_chunk_head_score_dots(o) for o in row] for row in operands.pop(p)]
            p = r - 1
            if 0 <= p < N_PAIRS:
                s1[p] = _pair_cumsum(gates.pop(p), tri)
        while next_slice < FF_SLICES:
            ffn_slice(next_slice)
            next_slice += 1

        gv = (_layer_norm(_gelu(field[COL_V])) * lnw_ref[...] + lnb_ref[...]).astype(_BF16)
        gu = _gelu(field[COL_U])
        n_blocks = TM // GM_BLOCK
        for g in range(GM_GROUPS):
            cols = slice(g * GM_CG, (g + 1) * GM_CG)
            rhs = jnp.concatenate([gv[nb * GM_BLOCK:(nb + 1) * GM_BLOCK, cols]
                                   for nb in range(n_blocks)], axis=1)
            sv = _dot(ws_ref[g], rhs)
            for nb in range(n_blocks):
                rows = slice(nb * GM_BLOCK, (nb + 1) * GM_BLOCK)
                sv_nb = sv[:, nb * GM_CG:(nb + 1) * GM_CG] + bs_ref[g]
                gm_ref[rows, cols] = (gu[rows, cols] * sv_nb).astype(_BF16)

        top, bot = slice(0, half), slice(half, TM)
        y_b = _dot(gm_ref[...], _weight(wpb_ref[...]))
        y_a = _dot(hg_ref[...], _weight(wpa_ref[...]))
        mix = (_sigmoid(field[COL_GA] + bgate_ref[0:1]) * y_a
               + _sigmoid(field[COL_GB] + bgate_ref[1:2]) * y_b).astype(_BF16)
        m = _dot(mix, _weight(wout_ref[...]))
        ffn_top = ffn_down(top)
        ffn_bot = ffn_down(bot)
        h1 = _layer_norm(DEEPNORM_ALPHA * x_ref[0] + g1 * m) * ln1w_ref[...] + ln1b_ref[...]
        h1_ref[slot] = h1
        sh2, sc2 = mod_ref[0, 3:4], mod_ref[0, 4:5]
        u2_ref[slot] = (_layer_norm(h1) * (1.0 + sc2) + sh2).astype(_BF16)
        ffn_finish(top, ffn_top)
        ffn_finish(bot, ffn_bot)

    tile_step(2 * s - 1, xa_ref, moda_ref, modf_ref, 0, 0)
    tile_step(2 * s, xb_ref, modb_ref, moda_ref, 1, 1)


def _block(x, mod, consts):
    n_tiles, _, d = x.shape
    n_batch = mod.shape[0]
    n_seq_tiles = n_tiles // n_batch
    last = n_tiles - 1
    assert n_tiles % 2 == 0 and n_seq_tiles % 2 == 0

    def const_spec(a):
        return pl.BlockSpec(a.shape, lambda s: (0,) * a.ndim, pipeline_mode=pl.Buffered(1))

    def tile_spec(offset):
        return pl.BlockSpec((1, TM, d), lambda s: (jnp.clip(2 * s + offset, 0, last), 0, 0))

    def mod_spec(offset):
        return pl.BlockSpec(
            (1, 6, d), lambda s: (jnp.clip(2 * s + offset, 0, last) // n_seq_tiles, 0, 0))

    return pl.pallas_call(
        functools.partial(_block_kernel, n_seq_tiles),
        grid=(n_tiles // 2 + 1,),
        in_specs=[tile_spec(-1), tile_spec(0), mod_spec(-1), mod_spec(0), mod_spec(-2)]
        + [const_spec(a) for a in consts],
        out_specs=pl.BlockSpec((2, TM, d), lambda s: (jnp.maximum(s - 1, 0), 0, 0)),
        out_shape=jax.ShapeDtypeStruct((n_tiles, TM, d), _F32),
        scratch_shapes=[
            pltpu.VMEM((HG_HEADS, HG_DK, HG_DK), _F32),
            pltpu.VMEM((TM, d), _BF16),
            pltpu.VMEM((TM, d), _BF16),
            pltpu.VMEM((2, TM, d), _F32),
            pltpu.VMEM((2, TM, d), _BF16),
            pltpu.VMEM((TM, D_FF), _BF16),
        ],
        compiler_params=pltpu.CompilerParams(
            dimension_semantics=("arbitrary",),
            vmem_limit_bytes=VMEM_LIMIT),
        name="block",
    )(x, x, mod, mod, mod, *consts)


def _chunk_tri(n):
    r = jnp.arange(n)[:, None]
    c = jnp.arange(n)[None, :]
    return ((r >= c) & (r // CHUNK == c // CHUNK)).astype(_BF16)


def kernel(x, c, w_ada, b_ada, w_in, b_gate, hgrn_lb_logits, hgrn_norm_w, w_proj_a, gmlp_ln_w,
           gmlp_ln_b, gmlp_ws, gmlp_bs, w_proj_b, w_out, ln1_w, ln1_b, w_ffn_in, w_ffn_out,
           ln2_w, ln2_b):
    assert w_ada.shape[0] == 1, "single-layer block"
    b, s, d = x.shape
    row = lambda a: a.reshape(1, -1)
    mod, lb = _prep(c, w_ada[0], b_ada[0], hgrn_lb_logits)
    mod = mod.transpose(1, 0, 2)

    pos = jnp.arange(GM_BLOCK) // CHUNK
    ws = jnp.where(pos[:, None] >= pos[None, :], gmlp_ws[0], 0.0).astype(_BF16)
    bs = jnp.broadcast_to(gmlp_bs[0][:, :, None], (GM_GROUPS, GM_BLOCK, GM_CG))

    p_in, p_a, p_b, p_out, p_ffn_in, p_ffn_out = _pack_weights(
        w_in[0], w_proj_a[0], w_proj_b[0], w_out[0], w_ffn_in[0], w_ffn_out[0])
    consts = (lb, p_in, _chunk_tri(TM), b_gate[0], row(hgrn_norm_w[0]),
              p_a, row(gmlp_ln_w[0]), row(gmlp_ln_b[0]), ws, bs,
              p_b, p_out, row(ln1_w[0]), row(ln1_b[0]),
              p_ffn_in, p_ffn_out, row(ln2_w[0]), row(ln2_b[0]))
    out = _block(x.reshape(b * s // TM, TM, d), mod, consts)
    return out.reshape(b, s, d)
```

```python
import functools

import jax
import jax.numpy as jnp
from jax import lax
from jax.experimental import pallas as pl
from jax.experimental.pallas import tpu as pltpu

D_MODEL = 1024
CHUNK = 64
SUB_CHUNK = 16
N_SUB = CHUNK // SUB_CHUNK
HG_DK = 128
HG_HEADS = D_MODEL // HG_DK
GM_BLOCK = 128
GM_GROUPS = 8
GM_CG = D_MODEL // GM_GROUPS
D_FF = 2816
DEEPNORM_ALPHA = 2.0 ** 0.25
LN_EPS = 1e-5
RMS_EPS = 1e-6

COL_Q, COL_F, COL_I, COL_G, COL_U, COL_V, COL_GA, COL_GB = (i * D_MODEL for i in range(8))

TM = 256
MXU_N = 256
HEAD_PAIR = MXU_N
N_CHUNKS = TM // CHUNK
N_PAIRS = HG_HEADS // 2
FF_SLICES = D_FF // MXU_N
FFN_LEAD = 4
FFN_PER_ROW = 1
PACK_STEPS = 8
VMEM_LIMIT = 63 * 1024 * 1024 + 512 * 1024

_F32 = jnp.float32
_BF16 = jnp.bfloat16


def _dot(a, b):
    return jnp.dot(a, b, preferred_element_type=_F32)


def _dot_nt(a, b):
    return lax.dot_general(a, b, (((1,), (1,)), ((), ())), preferred_element_type=_F32)


def _dot_tn(a, b):
    return lax.dot_general(a, b, (((0,), (0,)), ((), ())), preferred_element_type=_F32)


def _pack_kernel(*refs):
    n = len(refs) // 2
    for w_ref, o_ref in zip(refs[:n], refs[n:]):
        o_ref[...] = pltpu.bitcast(w_ref[...].astype(_BF16), jnp.uint32)


def _pack_weights(*ws):
    for w in ws:
        assert w.shape[0] % (16 * PACK_STEPS) == 0, w.shape
    rows = [w.shape[0] // PACK_STEPS for w in ws]
    return pl.pallas_call(
        _pack_kernel,
        grid=(PACK_STEPS,),
        in_specs=[pl.BlockSpec((r, w.shape[1]), lambda i: (i, 0)) for r, w in zip(rows, ws)],
        out_specs=[pl.BlockSpec((r // 2, w.shape[1]), lambda i: (i, 0)) for r, w in zip(rows, ws)],
        out_shape=[jax.ShapeDtypeStruct((w.shape[0] // 2, w.shape[1]), jnp.uint32) for w in ws],
        compiler_params=pltpu.CompilerParams(vmem_limit_bytes=VMEM_LIMIT),
        name="pack",
    )(*ws)


def _weight(packed):
    return pltpu.bitcast(packed, _BF16)


def _layer_norm(x):
    xc = x - jnp.mean(x, axis=-1, keepdims=True)
    return xc * lax.rsqrt(jnp.mean(xc * xc, axis=-1, keepdims=True) + LN_EPS)


def _sigmoid(x):
    return 1.0 / (1.0 + jnp.exp(-x))


def _silu(x):
    return x * _sigmoid(x)


def _gelu(x):
    return 0.5 * x * (1.0 + lax.erf(x * (2.0 ** -0.5)))


def _prep_kernel(c_ref, w_ref, b_ref, lbl_ref, mod_ref, lb_ref):
    cond = _silu(c_ref[...])
    mod_ref[0] = _dot(cond.astype(_BF16), w_ref[...].astype(_BF16)) + b_ref[0]
    logits = lbl_ref[...]
    e = jnp.exp(logits - jnp.max(logits, axis=0, keepdims=True))
    lb_ref[...] = e[0:1] / jnp.sum(e, axis=0, keepdims=True)


def _prep(c, w_ada, b_ada, lb_logits):
    b = c.shape[0]
    return pl.pallas_call(
        _prep_kernel,
        grid=(6,),
        in_specs=[
            pl.BlockSpec((b, D_MODEL), lambda j: (0, 0)),
            pl.BlockSpec((D_MODEL, D_MODEL), lambda j: (0, j)),
            pl.BlockSpec((1, 1, D_MODEL), lambda j: (j, 0, 0)),
            pl.BlockSpec(lb_logits.shape, lambda j: (0, 0)),
        ],
        out_specs=[
            pl.BlockSpec((1, b, D_MODEL), lambda j: (j, 0, 0)),
            pl.BlockSpec((1, D_MODEL), lambda j: (0, 0)),
        ],
        out_shape=[
            jax.ShapeDtypeStruct((6, b, D_MODEL), _F32),
            jax.ShapeDtypeStruct((1, D_MODEL), _F32),
        ],
        name="prep",
    )(c, w_ada, b_ada.reshape(6, 1, D_MODEL), lb_logits)


def _pair_project(u, w_in_ref, p):
    off = p * HEAD_PAIR
    zq, zf, zi, zg = (_dot(u, _weight(w_in_ref[:, col + off:col + off + HEAD_PAIR]))
                      for col in (COL_Q, COL_F, COL_I, COL_G))
    return zq, zf, zi.astype(_BF16), zg


def _pair_gates(zq, zf, lb):
    f = lb + (1.0 - lb) * _sigmoid(zf)
    l2f = jnp.log2(f)
    hi = l2f.astype(_BF16)
    lo = (l2f - hi.astype(_F32)).astype(_BF16)
    return _silu(zq), 1.0 - f, hi, lo


def _pair_cumsum(gates, tri):
    q, k, hi, lo = gates
    return q, k, _dot(tri, hi) + _dot(tri, lo)


def _chunk_head_operands(q, k, cum, v_b):
    cl = cum[CHUNK - 1:CHUNK]
    qd = (q * jnp.exp2(cum)).astype(_BF16)
    kd = (k * jnp.exp2(cl - cum)).astype(_BF16)
    dec = jnp.transpose(jnp.broadcast_to(jnp.exp2(cl), (HG_DK, HG_DK)))

    def keys(ref, hi):
        return (k[0:hi] * jnp.exp2(-cum[0:hi] if ref is None else ref - cum[0:hi])).astype(_BF16)

    def queries(ref, lo):
        return (q[lo:lo + SUB_CHUNK] * jnp.exp2(cum[lo:lo + SUB_CHUNK] - ref)).astype(_BF16)

    zeros = jnp.zeros((SUB_CHUNK, HG_DK), _BF16)
    scores = []
    for i0 in range(0, N_SUB, 2):
        lo0, lo1 = i0 * SUB_CHUNK, (i0 + 1) * SUB_CHUNK
        hi = lo1 + SUB_CHUNK
        ref0 = None if i0 == 0 else cum[lo0 - 1:lo0]
        ref1 = cum[lo1 - 1:lo1]
        q0 = qd[lo0:lo1] if i0 == 0 else queries(ref0, lo0)
        q1 = queries(ref1, lo1)
        lhs = jnp.concatenate([jnp.concatenate([q0, zeros], axis=1),
                               jnp.concatenate([zeros, q1], axis=1)], axis=0)
        rhs = jnp.concatenate([jnp.concatenate([keys(ref0, lo1), zeros], axis=0),
                               keys(ref1, hi)], axis=1)
        scores.append((lhs, rhs))
    return qd, kd, v_b, dec, scores


def _chunk_head_score_dots(operands):
    qd, kd, v_b, dec, pairs = operands
    upd = _dot_tn(kd, v_b)
    return qd, v_b, dec, upd, [_dot_nt(lhs, rhs) for lhs, rhs in pairs]


def _chunk_head_outputs(st, qd, v_b, scores):
    o_inter = _dot(qd, st.astype(_BF16))
    outs = []
    for j, sc in enumerate(scores):
        lo = 2 * j * SUB_CHUNK
        row = lax.broadcasted_iota(jnp.int32, sc.shape, 0) + lo
        col = lax.broadcasted_iota(jnp.int32, sc.shape, 1)
        sc = jnp.where(col <= row, sc, 0.0).astype(_BF16)
        outs.append(_dot(sc, v_b[0:sc.shape[1]]))
    return o_inter + jnp.concatenate(outs, axis=0)


def _block_kernel(n_seq_tiles,
                  xa_ref, xb_ref, moda_ref, modb_ref, modf_ref, lb_ref, w_in_ref, tri_ref, bgate_ref,
                  normw_ref, wpa_ref, lnw_ref, lnb_ref, ws_ref, bs_ref, wpb_ref, wout_ref, ln1w_ref,
                  ln1b_ref, w1_ref, w2_ref, ln2w_ref, ln2b_ref,
                  out_ref, st_ref, hg_ref, gm_ref, h1_ref, u2_ref, act_ref):
    s = pl.program_id(0)

    @pl.when(s == 0)
    def _():
        st_ref[...] = jnp.zeros_like(st_ref)
        h1_ref[...] = jnp.zeros_like(h1_ref)
        u2_ref[...] = jnp.zeros_like(u2_ref)

    half = TM // 2
    tri = tri_ref[...]
    norm_w = normw_ref[...]

    def chunk_head_slices(c, hh):
        return (slice(c * CHUNK, (c + 1) * CHUNK), slice(hh * HG_DK, (hh + 1) * HG_DK))

    def tile_step(tile, x_ref, mod_ref, modf_ref, slot, out_idx):
        prev = 1 - slot
        keep = jnp.where(tile % n_seq_tiles == 0, 0.0, 1.0).astype(_F32)

        def ffn_slice(i):
            u2 = u2_ref[prev]
            a = _dot(u2, _weight(w1_ref[:, i * MXU_N:(i + 1) * MXU_N]))
            b = _dot(u2, _weight(w1_ref[:, D_FF + i * MXU_N:D_FF + (i + 1) * MXU_N]))
            act_ref[:, i * MXU_N:(i + 1) * MXU_N] = (_silu(a) * b).astype(_BF16)

        def ffn_down(rows):
            return _dot(act_ref[rows, :], _weight(w2_ref[...]))

        def ffn_finish(rows, ffn):
            g2 = modf_ref[0, 5:6]
            r = DEEPNORM_ALPHA * h1_ref[prev, rows, :] + g2 * ffn
            out_ref[out_idx, rows, :] = _layer_norm(r) * ln2w_ref[...] + ln2b_ref[...]

        sh1, sc1, g1 = mod_ref[0, 0:1], mod_ref[0, 1:2], mod_ref[0, 2:3]
        for i in range(FFN_LEAD):
            ffn_slice(i)
        u = (_layer_norm(x_ref[0]) * (1.0 + sc1) + sh1).astype(_BF16)

        def stage2(z, gates):
            q, k, cum = gates
            return [[_chunk_head_operands(q[sl], k[sl], cum[sl], z[2][sl])
                     for sl in (chunk_head_slices(c, hh) for hh in range(2))]
                    for c in range(N_CHUNKS)]

        def stage3(p, s2):
            outs = []
            for hh in range(2):
                st = st_ref[2 * p + hh] * keep
                col = []
                for c in range(N_CHUNKS):
                    qd, v_b, dec, upd, scores = s2[c][hh]
                    col.append(_chunk_head_outputs(st, qd, v_b, scores))
                    st = st * dec + upd
                st_ref[2 * p + hh] = st
                outs.append(col)
            return outs

        def stage4(p, z, s3):
            for hh in range(2):
                hd = 2 * p + hh
                for c in range(N_CHUNKS):
                    sl = chunk_head_slices(c, hh)
                    o = s3[hh][c]
                    o = o * lax.rsqrt(jnp.mean(o * o, axis=-1, keepdims=True) + RMS_EPS) * norm_w
                    hg_ref[sl[0], hd * HG_DK:(hd + 1) * HG_DK] = (
                        o * _silu(z[3][sl])).astype(_BF16)

        field = {}
        z, s1, s2, operands, gates = {}, {}, {}, {}, {}
        fillers = (COL_V, COL_U, COL_GA, COL_GB)
        consume = {
            COL_V: lambda zv: (_layer_norm(_gelu(zv)) * lnw_ref[...] + lnb_ref[...]).astype(_BF16),
            COL_U: _gelu,
            COL_GA: lambda zg: _sigmoid(zg + bgate_ref[0:1]),
            COL_GB: lambda zg: _sigmoid(zg + bgate_ref[1:2]),
        }
        next_slice = FFN_LEAD
        for r in range(N_PAIRS + len(fillers)):
            p = r - 2
            if 0 <= p < N_PAIRS:
                operands[p] = stage2(z[p], s1.pop(p))
            p = r - 1
            if 0 <= p < N_PAIRS:
                lb = lb_ref[:, p * HEAD_PAIR:(p + 1) * HEAD_PAIR]
                gates[p] = _pair_gates(z[p][0], z[p][1], lb)
            if r < N_PAIRS:
                z[r] = _pair_project(u, w_in_ref, r)
            else:
                col = fillers[r - N_PAIRS]
                field[col] = consume[col](_dot(u, _weight(w_in_ref[:, col:col + D_MODEL])))
            for _ in range(FFN_PER_ROW):
                if next_slice < FF_SLICES:
                    ffn_slice(next_slice)
                    next_slice += 1
            p = r - 3
            if 0 <= p < N_PAIRS:
                stage4(p, z.pop(p), stage3(p, s2.pop(p)))
            p = r - 2
            if 0 <= p < N_PAIRS:
                s2[p] = [[_chunk_head_score_dots(o) for o in row] for row in operands.pop(p)]
            p = r - 1
            if 0 <= p < N_PAIRS:
                s1[p] = _pair_cumsum(gates.pop(p), tri)
        while next_slice < FF_SLICES:
            ffn_slice(next_slice)
            next_slice += 1

        gv, gu = field[COL_V], field[COL_U]
        n_blocks = TM // GM_BLOCK
        for g in range(GM_GROUPS):
            cols = slice(g * GM_CG, (g + 1) * GM_CG)
            rhs = jnp.concatenate([gv[nb * GM_BLOCK:(nb + 1) * GM_BLOCK, cols]
                                   for nb in range(n_blocks)], axis=1)
            sv = _dot(ws_ref[g], rhs)
            for nb in range(n_blocks):
                rows = slice(nb * GM_BLOCK, (nb + 1) * GM_BLOCK)
                sv_nb = sv[:, nb * GM_CG:(nb + 1) * GM_CG] + bs_ref[g]
                gm_ref[rows, cols] = (gu[rows, cols] * sv_nb).astype(_BF16)

        top, bot = slice(0, half), slice(half, TM)
        y_b = _dot(gm_ref[...], _weight(wpb_ref[...]))
        y_a = _dot(hg_ref[...], _weight(wpa_ref[...]))
        mix = (field[COL_GA] * y_a + field[COL_GB] * y_b).astype(_BF16)
        m = _dot(mix, _weight(wout_ref[...]))
        ffn_top = ffn_down(top)
        ffn_bot = ffn_down(bot)
        h1 = _layer_norm(DEEPNORM_ALPHA * x_ref[0] + g1 * m) * ln1w_ref[...] + ln1b_ref[...]
        h1_ref[slot] = h1
        sh2, sc2 = mod_ref[0, 3:4], mod_ref[0, 4:5]
        u2_ref[slot] = (_layer_norm(h1) * (1.0 + sc2) + sh2).astype(_BF16)
        ffn_finish(top, ffn_top)
        ffn_finish(bot, ffn_bot)

    tile_step(2 * s - 1, xa_ref, moda_ref, modf_ref, 0, 0)
    tile_step(2 * s, xb_ref, modb_ref, moda_ref, 1, 1)


def _block(x, mod, consts):
    n_tiles, _, d = x.shape
    n_batch = mod.shape[0]
    n_seq_tiles = n_tiles // n_batch
    last = n_tiles - 1
    assert n_tiles % 2 == 0 and n_seq_tiles % 2 == 0

    def const_spec(a):
        return pl.BlockSpec(a.shape, lambda s: (0,) * a.ndim, pipeline_mode=pl.Buffered(1))

    def tile_spec(offset):
        return pl.BlockSpec((1, TM, d), lambda s: (jnp.clip(2 * s + offset, 0, last), 0, 0))

    def mod_spec(offset):
        return pl.BlockSpec(
            (1, 6, d), lambda s: (jnp.clip(2 * s + offset, 0, last) // n_seq_tiles, 0, 0))

    return pl.pallas_call(
        functools.partial(_block_kernel, n_seq_tiles),
        grid=(n_tiles // 2 + 1,),
        in_specs=[tile_spec(-1), tile_spec(0), mod_spec(-1), mod_spec(0), mod_spec(-2)]
        + [const_spec(a) for a in consts],
        out_specs=pl.BlockSpec((2, TM, d), lambda s: (jnp.maximum(s - 1, 0), 0, 0)),
        out_shape=jax.ShapeDtypeStruct((n_tiles, TM, d), _F32),
        scratch_shapes=[
            pltpu.VMEM((HG_HEADS, HG_DK, HG_DK), _F32),
            pltpu.VMEM((TM, d), _BF16),
            pltpu.VMEM((TM, d), _BF16),
            pltpu.VMEM((2, TM, d), _F32),
            pltpu.VMEM((2, TM, d), _BF16),
            pltpu.VMEM((TM, D_FF), _BF16),
        ],
        compiler_params=pltpu.CompilerParams(
            dimension_semantics=("arbitrary",),
            vmem_limit_bytes=VMEM_LIMIT),
        name="block",
    )(x, x, mod, mod, mod, *consts)


def _chunk_tri(n):
    r = jnp.arange(n)[:, None]
    c = jnp.arange(n)[None, :]
    return ((r >= c) & (r // CHUNK == c // CHUNK)).astype(_BF16)


def kernel(x, c, w_ada, b_ada, w_in, b_gate, hgrn_lb_logits, hgrn_norm_w, w_proj_a, gmlp_ln_w,
           gmlp_ln_b, gmlp_ws, gmlp_bs, w_proj_b, w_out, ln1_w, ln1_b, w_ffn_in, w_ffn_out,
           ln2_w, ln2_b):
    assert w_ada.shape[0] == 1, "single-layer block"
    b, s, d = x.shape
    row = lambda a: a.reshape(1, -1)
    mod, lb = _prep(c, w_ada[0], b_ada[0], hgrn_lb_logits)
    mod = mod.transpose(1, 0, 2)

    pos = jnp.arange(GM_BLOCK) // CHUNK
    ws = jnp.where(pos[:, None] >= pos[None, :], gmlp_ws[0], 0.0).astype(_BF16)
    bs = jnp.broadcast_to(gmlp_bs[0][:, :, None], (GM_GROUPS, GM_BLOCK, GM_CG))

    p_in, p_a, p_b, p_out, p_ffn_in, p_ffn_out = _pack_weights(
        w_in[0], w_proj_a[0], w_proj_b[0], w_out[0], w_ffn_in[0], w_ffn_out[0])
    consts = (lb, p_in, _chunk_tri(TM), b_gate[0], row(hgrn_norm_w[0]),
              p_a, row(gmlp_ln_w[0]), row(gmlp_ln_b[0]), ws, bs,
              p_b, p_out, row(ln1_w[0]), row(ln1_b[0]),
              p_ffn_in, p_ffn_out, row(ln2_w[0]), row(ln2_b[0]))
    out = _block(x.reshape(b * s // TM, TM, d), mod, consts)
    return out.reshape(b, s, d)
```

```python
import functools

import jax
import jax.numpy as jnp
from jax import lax
from jax.experimental import pallas as pl
from jax.experimental.pallas import tpu as pltpu

D_MODEL = 1024
CHUNK = 64
SUB_CHUNK = 16
N_SUB = CHUNK // SUB_CHUNK
HG_DK = 128
HG_HEADS = D_MODEL // HG_DK
GM_BLOCK = 128
GM_GROUPS = 8
GM_CG = D_MODEL // GM_GROUPS
D_FF = 2816
DEEPNORM_ALPHA = 2.0 ** 0.25
LN_EPS = 1e-5
RMS_EPS = 1e-6

COL_Q, COL_F, COL_I, COL_G, COL_U, COL_V, COL_GA, COL_GB = (i * D_MODEL for i in range(8))

TM = 256
MXU_N = 256
HEAD_PAIR = MXU_N
N_CHUNKS = TM // CHUNK
N_PAIRS = HG_HEADS // 2
FF_SLICES = D_FF // MXU_N
FFN_LEAD = 5
FFN_PER_ROW = 1
PACK_STEPS = 8
VMEM_LIMIT = 63 * 1024 * 1024 + 512 * 1024

_F32 = jnp.float32
_BF16 = jnp.bfloat16


def _dot(a, b):
    return jnp.dot(a, b, preferred_element_type=_F32)


def _dot_nt(a, b):
    return lax.dot_general(a, b, (((1,), (1,)), ((), ())), preferred_element_type=_F32)


def _dot_tn(a, b):
    return lax.dot_general(a, b, (((0,), (0,)), ((), ())), preferred_element_type=_F32)


def _pack_kernel(*refs):
    n = len(refs) // 2
    for w_ref, o_ref in zip(refs[:n], refs[n:]):
        o_ref[...] = pltpu.bitcast(w_ref[...].astype(_BF16), jnp.uint32)


def _pack_weights(*ws):
    for w in ws:
        assert w.shape[0] % (16 * PACK_STEPS) == 0, w.shape
    rows = [w.shape[0] // PACK_STEPS for w in ws]
    return pl.pallas_call(
        _pack_kernel,
        grid=(PACK_STEPS,),
        in_specs=[pl.BlockSpec((r, w.shape[1]), lambda i: (i, 0)) for r, w in zip(rows, ws)],
        out_specs=[pl.BlockSpec((r // 2, w.shape[1]), lambda i: (i, 0)) for r, w in zip(rows, ws)],
        out_shape=[jax.ShapeDtypeStruct((w.shape[0] // 2, w.shape[1]), jnp.uint32) for w in ws],
        compiler_params=pltpu.CompilerParams(vmem_limit_bytes=VMEM_LIMIT),
        name="pack",
    )(*ws)


def _weight(packed):
    return pltpu.bitcast(packed, _BF16)


def _layer_norm(x):
    xc = x - jnp.mean(x, axis=-1, keepdims=True)
    return xc * lax.rsqrt(jnp.mean(xc * xc, axis=-1, keepdims=True) + LN_EPS)


def _sigmoid(x):
    return 1.0 / (1.0 + jnp.exp(-x))


def _silu(x):
    return x * _sigmoid(x)


def _gelu(x):
    return 0.5 * x * (1.0 + lax.erf(x * (2.0 ** -0.5)))


def _prep_kernel(c_ref, w_ref, b_ref, lbl_ref, mod_ref, lb_ref):
    cond = _silu(c_ref[...])
    mod_ref[0] = _dot(cond.astype(_BF16), w_ref[...].astype(_BF16)) + b_ref[0]
    logits = lbl_ref[...]
    e = jnp.exp(logits - jnp.max(logits, axis=0, keepdims=True))
    lb_ref[...] = e[0:1] / jnp.sum(e, axis=0, keepdims=True)


def _prep(c, w_ada, b_ada, lb_logits):
    b = c.shape[0]
    return pl.pallas_call(
        _prep_kernel,
        grid=(6,),
        in_specs=[
            pl.BlockSpec((b, D_MODEL), lambda j: (0, 0)),
            pl.BlockSpec((D_MODEL, D_MODEL), lambda j: (0, j)),
            pl.BlockSpec((1, 1, D_MODEL), lambda j: (j, 0, 0)),
            pl.BlockSpec(lb_logits.shape, lambda j: (0, 0)),
        ],
        out_specs=[
            pl.BlockSpec((1, b, D_MODEL), lambda j: (j, 0, 0)),
            pl.BlockSpec((1, D_MODEL), lambda j: (0, 0)),
        ],
        out_shape=[
            jax.ShapeDtypeStruct((6, b, D_MODEL), _F32),
            jax.ShapeDtypeStruct((1, D_MODEL), _F32),
        ],
        name="prep",
    )(c, w_ada, b_ada.reshape(6, 1, D_MODEL), lb_logits)


def _pair_project(u, w_in_ref, p):
    off = p * HEAD_PAIR
    zq, zf, zi, zg = (_dot(u, _weight(w_in_ref[:, col + off:col + off + HEAD_PAIR]))
                      for col in (COL_Q, COL_F, COL_I, COL_G))
    return zq, zf, zi.astype(_BF16), zg


def _pair_gates(zq, zf, lb):
    f = lb + (1.0 - lb) * _sigmoid(zf)
    l2f = jnp.log2(f)
    hi = l2f.astype(_BF16)
    lo = (l2f - hi.astype(_F32)).astype(_BF16)
    return _silu(zq), 1.0 - f, hi, lo


def _pair_cumsum(gates, tri):
    q, k, hi, lo = gates
    return q, k, _dot(tri, hi) + _dot(tri, lo)


def _chunk_head_operands(q, k, cum, v_b):
    cl = cum[CHUNK - 1:CHUNK]
    qd = (q * jnp.exp2(cum)).astype(_BF16)
    kd = (k * jnp.exp2(cl - cum)).astype(_BF16)
    dec = jnp.transpose(jnp.broadcast_to(jnp.exp2(cl), (HG_DK, HG_DK)))

    def keys(ref, hi):
        return (k[0:hi] * jnp.exp2(-cum[0:hi] if ref is None else ref - cum[0:hi])).astype(_BF16)

    def queries(ref, lo):
        return (q[lo:lo + SUB_CHUNK] * jnp.exp2(cum[lo:lo + SUB_CHUNK] - ref)).astype(_BF16)

    zeros = jnp.zeros((SUB_CHUNK, HG_DK), _BF16)
    scores = []
    for i0 in range(0, N_SUB, 2):
        lo0, lo1 = i0 * SUB_CHUNK, (i0 + 1) * SUB_CHUNK
        hi = lo1 + SUB_CHUNK
        ref0 = None if i0 == 0 else cum[lo0 - 1:lo0]
        ref1 = cum[lo1 - 1:lo1]
        q0 = qd[lo0:lo1] if i0 == 0 else queries(ref0, lo0)
        q1 = queries(ref1, lo1)
        lhs = jnp.concatenate([jnp.concatenate([q0, zeros], axis=1),
                               jnp.concatenate([zeros, q1], axis=1)], axis=0)
        rhs = jnp.concatenate([jnp.concatenate([keys(ref0, lo1), zeros], axis=0),
                               keys(ref1, hi)], axis=1)
        scores.append((lhs, rhs))
    return qd, kd, v_b, dec, scores


def _chunk_head_score_dots(operands):
    qd, kd, v_b, dec, pairs = operands
    upd = _dot_tn(kd, v_b)
    return qd, v_b, dec, upd, [_dot_nt(lhs, rhs) for lhs, rhs in pairs]


def _chunk_head_outputs(st, qd, v_b, scores):
    o_inter = _dot(qd, st.astype(_BF16))
    outs = []
    for j, sc in enumerate(scores):
        lo = 2 * j * SUB_CHUNK
        row = lax.broadcasted_iota(jnp.int32, sc.shape, 0) + lo
        col = lax.broadcasted_iota(jnp.int32, sc.shape, 1)
        sc = jnp.where(col <= row, sc, 0.0).astype(_BF16)
        outs.append(_dot(sc, v_b[0:sc.shape[1]]))
    return o_inter + jnp.concatenate(outs, axis=0)


def _block_kernel(n_seq_tiles,
                  xa_ref, xb_ref, moda_ref, modb_ref, modf_ref, lb_ref, w_in_ref, tri_ref, bgate_ref,
                  normw_ref, wpa_ref, lnw_ref, lnb_ref, ws_ref, bs_ref, wpb_ref, wout_ref, ln1w_ref,
                  ln1b_ref, w1_ref, w2_ref, ln2w_ref, ln2b_ref,
                  out_ref, st_ref, hg_ref, gm_ref, h1_ref, u2_ref, act_ref):
    s = pl.program_id(0)

    @pl.when(s == 0)
    def _():
        st_ref[...] = jnp.zeros_like(st_ref)
        h1_ref[...] = jnp.zeros_like(h1_ref)
        u2_ref[...] = jnp.zeros_like(u2_ref)

    half = TM // 2
    tri = tri_ref[...]
    norm_w = normw_ref[...]

    def chunk_head_slices(c, hh):
        return (slice(c * CHUNK, (c + 1) * CHUNK), slice(hh * HG_DK, (hh + 1) * HG_DK))

    def tile_step(tile, x_ref, mod_ref, modf_ref, slot, out_idx):
        prev = 1 - slot
        keep = jnp.where(tile % n_seq_tiles == 0, 0.0, 1.0).astype(_F32)

        def ffn_slice(i):
            u2 = u2_ref[prev]
            a = _dot(u2, _weight(w1_ref[:, i * MXU_N:(i + 1) * MXU_N]))
            b = _dot(u2, _weight(w1_ref[:, D_FF + i * MXU_N:D_FF + (i + 1) * MXU_N]))
            act_ref[:, i * MXU_N:(i + 1) * MXU_N] = (_silu(a) * b).astype(_BF16)

        def ffn_down(rows):
            return _dot(act_ref[rows, :], _weight(w2_ref[...]))

        def ffn_finish(rows, ffn):
            g2 = modf_ref[0, 5:6]
            r = DEEPNORM_ALPHA * h1_ref[prev, rows, :] + g2 * ffn
            out_ref[out_idx, rows, :] = _layer_norm(r) * ln2w_ref[...] + ln2b_ref[...]

        sh1, sc1, g1 = mod_ref[0, 0:1], mod_ref[0, 1:2], mod_ref[0, 2:3]
        for i in range(FFN_LEAD):
            ffn_slice(i)
        u = (_layer_norm(x_ref[0]) * (1.0 + sc1) + sh1).astype(_BF16)

        def stage2(z, gates):
            q, k, cum = gates
            return [[_chunk_head_operands(q[sl], k[sl], cum[sl], z[2][sl])
                     for sl in (chunk_head_slices(c, hh) for hh in range(2))]
                    for c in range(N_CHUNKS)]

        def stage3(p, s2):
            outs = []
            for hh in range(2):
                st = st_ref[2 * p + hh] * keep
                col = []
                for c in range(N_CHUNKS):
                    qd, v_b, dec, upd, scores = s2[c][hh]
                    col.append(_chunk_head_outputs(st, qd, v_b, scores))
                    st = st * dec + upd
                st_ref[2 * p + hh] = st
                outs.append(col)
            return outs

        def stage4(p, z, s3):
            for hh in range(2):
                hd = 2 * p + hh
                for c in range(N_CHUNKS):
                    sl = chunk_head_slices(c, hh)
                    o = s3[hh][c]
                    o = o * lax.rsqrt(jnp.mean(o * o, axis=-1, keepdims=True) + RMS_EPS) * norm_w
                    hg_ref[sl[0], hd * HG_DK:(hd + 1) * HG_DK] = (
                        o * _silu(z[3][sl])).astype(_BF16)

        field = {}
        z, s1, s2, operands, gates = {}, {}, {}, {}, {}
        fillers = (COL_V, COL_U, COL_GA, COL_GB)
        next_slice = FFN_LEAD
        for r in range(N_PAIRS + len(fillers)):
            p = r - 2
            if 0 <= p < N_PAIRS:
                operands[p] = stage2(z[p], s1.pop(p))
            p = r - 1
            if 0 <= p < N_PAIRS:
                lb = lb_ref[:, p * HEAD_PAIR:(p + 1) * HEAD_PAIR]
                gates[p] = _pair_gates(z[p][0], z[p][1], lb)
            if r < N_PAIRS:
                z[r] = _pair_project(u, w_in_ref, r)
            else:
                col = fillers[r - N_PAIRS]
                field[col] = _dot(u, _weight(w_in_ref[:, col:col + D_MODEL]))
            for _ in range(FFN_PER_ROW):
                if next_slice < FF_SLICES:
                    ffn_slice(next_slice)
                    next_slice += 1
            p = r - 3
            if 0 <= p < N_PAIRS:
                stage4(p, z.pop(p), stage3(p, s2.pop(p)))
            p = r - 2
            if 0 <= p < N_PAIRS:
                s2[p] = [[_chunk_head_score_dots(o) for o in row] for row in operands.pop(p)]
            p = r - 1
            if 0 <= p < N_PAIRS:
                s1[p] = _pair_cumsum(gates.pop(p), tri)
        while next_slice < FF_SLICES:
            ffn_slice(next_slice)
            next_slice += 1

        gv = (_layer_norm(_gelu(field[COL_V])) * lnw_ref[...] + lnb_ref[...]).astype(_BF16)
        gu = _gelu(field[COL_U])
        n_blocks = TM // GM_BLOCK
        for g in range(GM_GROUPS):
            cols = slice(g * GM_CG, (g + 1) * GM_CG)
            rhs = jnp.concatenate([gv[nb * GM_BLOCK:(nb + 1) * GM_BLOCK, cols]
                                   for nb in range(n_blocks)], axis=1)
            sv = _dot(ws_ref[g], rhs)
            for nb in range(n_blocks):
                rows = slice(nb * GM_BLOCK, (nb + 1) * GM_BLOCK)
                sv_nb = sv[:, nb * GM_CG:(nb + 1) * GM_CG] + bs_ref[g]
                gm_ref[rows, cols] = (gu[rows, cols] * sv_nb).astype(_BF16)

        top, bot = slice(0, half), slice(half, TM)
        y_b = _dot(gm_ref[...], _weight(wpb_ref[...]))
        y_a = _dot(hg_ref[...], _weight(wpa_ref[...]))
        mix = (_sigmoid(field[COL_GA] + bgate_ref[0:1]) * y_a
               + _sigmoid(field[COL_GB] + bgate_ref[1:2]) * y_b).astype(_BF16)
        m = _dot(mix, _weight(wout_ref[...]))
        ffn_top = ffn_down(top)
        ffn_bot = ffn_down(bot)
        h1 = _layer_norm(DEEPNORM_ALPHA * x_ref[0] + g1 * m) * ln1w_ref[...] + ln1b_ref[...]
        h1_ref[slot] = h1
        sh2, sc2 = mod_ref[0, 3:4], mod_ref[0, 4:5]
        u2_ref[slot] = (_layer_norm(h1) * (1.0 + sc2) + sh2).astype(_BF16)
        ffn_finish(top, ffn_top)
        ffn_finish(bot, ffn_bot)

    tile_step(2 * s - 1, xa_ref, moda_ref, modf_ref, 0, 0)
    tile_step(2 * s, xb_ref, modb_ref, moda_ref, 1, 1)


def _block(x, mod, consts):
    n_tiles, _, d = x.shape
    n_batch = mod.shape[0]
    n_seq_tiles = n_tiles // n_batch
    last = n_tiles - 1
    assert n_tiles % 2 == 0 and n_seq_tiles % 2 == 0

    def const_spec(a):
        return pl.BlockSpec(a.shape, lambda s: (0,) * a.ndim, pipeline_mode=pl.Buffered(1))

    def tile_spec(offset):
        return pl.BlockSpec((1, TM, d), lambda s: (jnp.clip(2 * s + offset, 0, last), 0, 0))

    def mod_spec(offset):
        return pl.BlockSpec(
            (1, 6, d), lambda s: (jnp.clip(2 * s + offset, 0, last) // n_seq_tiles, 0, 0))

    return pl.pallas_call(
        functools.partial(_block_kernel, n_seq_tiles),
        grid=(n_tiles // 2 + 1,),
        in_specs=[tile_spec(-1), tile_spec(0), mod_spec(-1), mod_spec(0), mod_spec(-2)]
        + [const_spec(a) for a in consts],
        out_specs=pl.BlockSpec((2, TM, d), lambda s: (jnp.maximum(s - 1, 0), 0, 0)),
        out_shape=jax.ShapeDtypeStruct((n_tiles, TM, d), _F32),
        scratch_shapes=[
            pltpu.VMEM((HG_HEADS, HG_DK, HG_DK), _F32),
            pltpu.VMEM((TM, d), _BF16),
            pltpu.VMEM((TM, d), _BF16),
            pltpu.VMEM((2, TM, d), _F32),
            pltpu.VMEM((2, TM, d), _BF16),
            pltpu.VMEM((TM, D_FF), _BF16),
        ],
        compiler_params=pltpu.CompilerParams(
            dimension_semantics=("arbitrary",),
            vmem_limit_bytes=VMEM_LIMIT),
        name="block",
    )(x, x, mod, mod, mod, *consts)


def _chunk_tri(n):
    r = jnp.arange(n)[:, None]
    c = jnp.arange(n)[None, :]
    return ((r >= c) & (r // CHUNK == c // CHUNK)).astype(_BF16)


def kernel(x, c, w_ada, b_ada, w_in, b_gate, hgrn_lb_logits, hgrn_norm_w, w_proj_a, gmlp_ln_w,
           gmlp_ln_b, gmlp_ws, gmlp_bs, w_proj_b, w_out, ln1_w, ln1_b, w_ffn_in, w_ffn_out,
           ln2_w, ln2_b):
    assert w_ada.shape[0] == 1, "single-layer block"
    b, s, d = x.shape
    row = lambda a: a.reshape(1, -1)
    mod, lb = _prep(c, w_ada[0], b_ada[0], hgrn_lb_logits)
    mod = mod.transpose(1, 0, 2)

    pos = jnp.arange(GM_BLOCK) // CHUNK
    ws = jnp.where(pos[:, None] >= pos[None, :], gmlp_ws[0], 0.0).astype(_BF16)
    bs = jnp.broadcast_to(gmlp_bs[0][:, :, None], (GM_GROUPS, GM_BLOCK, GM_CG))

    p_in, p_a, p_b, p_out, p_ffn_in, p_ffn_out = _pack_weights(
        w_in[0], w_proj_a[0], w_proj_b[0], w_out[0], w_ffn_in[0], w_ffn_out[0])
    consts = (lb, p_in, _chunk_tri(TM), b_gate[0], row(hgrn_norm_w[0]),
              p_a, row(gmlp_ln_w[0]), row(gmlp_ln_b[0]), ws, bs,
              p_b, p_out, row(ln1_w[0]), row(ln1_b[0]),
              p_ffn_in, p_ffn_out, row(ln2_w[0]), row(ln2_b[0]))
    out = _block(x.reshape(b * s // TM, TM, d), mod, consts)
    return out.reshape(b, s, d)
```

```python
import functools

import jax
import jax.numpy as jnp
from jax import lax
from jax.experimental import pallas as pl
from jax.experimental.pallas import tpu as pltpu

D_MODEL = 1024
CHUNK = 64
SUB_CHUNK = 16
N_SUB = CHUNK // SUB_CHUNK
HG_DK = 128
HG_HEADS = D_MODEL // HG_DK
GM_BLOCK = 128
GM_GROUPS = 8
GM_CG = D_MODEL // GM_GROUPS
D_FF = 2816
DEEPNORM_ALPHA = 2.0 ** 0.25
LN_EPS = 1e-5
RMS_EPS = 1e-6

COL_Q, COL_F, COL_I, COL_G, COL_U, COL_V, COL_GA, COL_GB = (i * D_MODEL for i in range(8))

TM = 256
MXU_N = 256
HEAD_PAIR = MXU_N
N_CHUNKS = TM // CHUNK
N_PAIRS = HG_HEADS // 2
FF_SLICES = D_FF // MXU_N
FFN_LEAD = 4
FFN_PER_ROW = 1
PACK_STEPS = 8
VMEM_LIMIT = 63 * 1024 * 1024 + 512 * 1024

_F32 = jnp.float32
_BF16 = jnp.bfloat16


def _dot(a, b):
    return jnp.dot(a, b, preferred_element_type=_F32)


def _dot_nt(a, b):
    return lax.dot_general(a, b, (((1,), (1,)), ((), ())), preferred_element_type=_F32)


def _dot_tn(a, b):
    return lax.dot_general(a, b, (((0,), (0,)), ((), ())), preferred_element_type=_F32)


def _pack_kernel(*refs):
    n = len(refs) // 2
    for w_ref, o_ref in zip(refs[:n], refs[n:]):
        o_ref[...] = pltpu.bitcast(w_ref[...].astype(_BF16), jnp.uint32)


def _pack_weights(*ws):
    for w in ws:
        assert w.shape[0] % (16 * PACK_STEPS) == 0, w.shape
    rows = [w.shape[0] // PACK_STEPS for w in ws]
    return pl.pallas_call(
        _pack_kernel,
        grid=(PACK_STEPS,),
        in_specs=[pl.BlockSpec((r, w.shape[1]), lambda i: (i, 0)) for r, w in zip(rows, ws)],
        out_specs=[pl.BlockSpec((r // 2, w.shape[1]), lambda i: (i, 0)) for r, w in zip(rows, ws)],
        out_shape=[jax.ShapeDtypeStruct((w.shape[0] // 2, w.shape[1]), jnp.uint32) for w in ws],
        compiler_params=pltpu.CompilerParams(vmem_limit_bytes=VMEM_LIMIT),
        name="pack",
    )(*ws)


def _weight(packed):
    return pltpu.bitcast(packed, _BF16)


def _layer_norm(x):
    xc = x - jnp.mean(x, axis=-1, keepdims=True)
    return xc * lax.rsqrt(jnp.mean(xc * xc, axis=-1, keepdims=True) + LN_EPS)


def _by_row_halves(fn, *arrays):
    half = arrays[0].shape[0] // 2
    return jnp.concatenate([fn(*(a[0:half] for a in arrays)),
                            fn(*(a[half:] for a in arrays))], axis=0)


def _sigmoid(x):
    return 1.0 / (1.0 + jnp.exp(-x))


def _silu(x):
    return x * _sigmoid(x)


def _gelu(x):
    return 0.5 * x * (1.0 + lax.erf(x * (2.0 ** -0.5)))


def _prep_kernel(c_ref, w_ref, b_ref, lbl_ref, mod_ref, lb_ref):
    cond = _silu(c_ref[...])
    mod_ref[0] = _dot(cond.astype(_BF16), w_ref[...].astype(_BF16)) + b_ref[0]
    logits = lbl_ref[...]
    e = jnp.exp(logits - jnp.max(logits, axis=0, keepdims=True))
    lb_ref[...] = e[0:1] / jnp.sum(e, axis=0, keepdims=True)


def _prep(c, w_ada, b_ada, lb_logits):
    b = c.shape[0]
    return pl.pallas_call(
        _prep_kernel,
        grid=(6,),
        in_specs=[
            pl.BlockSpec((b, D_MODEL), lambda j: (0, 0)),
            pl.BlockSpec((D_MODEL, D_MODEL), lambda j: (0, j)),
            pl.BlockSpec((1, 1, D_MODEL), lambda j: (j, 0, 0)),
            pl.BlockSpec(lb_logits.shape, lambda j: (0, 0)),
        ],
        out_specs=[
            pl.BlockSpec((1, b, D_MODEL), lambda j: (j, 0, 0)),
            pl.BlockSpec((1, D_MODEL), lambda j: (0, 0)),
        ],
        out_shape=[
            jax.ShapeDtypeStruct((6, b, D_MODEL), _F32),
            jax.ShapeDtypeStruct((1, D_MODEL), _F32),
        ],
        name="prep",
    )(c, w_ada, b_ada.reshape(6, 1, D_MODEL), lb_logits)


def _pair_project(u, w_in_ref, p):
    off = p * HEAD_PAIR
    zq, zf, zi, zg = (_dot(u, _weight(w_in_ref[:, col + off:col + off + HEAD_PAIR]))
                      for col in (COL_Q, COL_F, COL_I, COL_G))
    return zq, zf, zi.astype(_BF16), zg


def _pair_gates(zq, zf, lb):
    f = lb + (1.0 - lb) * _sigmoid(zf)
    l2f = jnp.log2(f)
    hi = l2f.astype(_BF16)
    lo = (l2f - hi.astype(_F32)).astype(_BF16)
    return _silu(zq), 1.0 - f, hi, lo


def _pair_cumsum(gates, tri):
    q, k, hi, lo = gates
    return q, k, _dot(tri, hi) + _dot(tri, lo)


def _chunk_head_operands(q, k, cum, v_b):
    cl = cum[CHUNK - 1:CHUNK]
    qd = (q * jnp.exp2(cum)).astype(_BF16)
    kd = (k * jnp.exp2(cl - cum)).astype(_BF16)
    dec = jnp.transpose(jnp.broadcast_to(jnp.exp2(cl), (HG_DK, HG_DK)))

    def keys(ref, hi):
        return (k[0:hi] * jnp.exp2(-cum[0:hi] if ref is None else ref - cum[0:hi])).astype(_BF16)

    def queries(ref, lo):
        return (q[lo:lo + SUB_CHUNK] * jnp.exp2(cum[lo:lo + SUB_CHUNK] - ref)).astype(_BF16)

    zeros = jnp.zeros((SUB_CHUNK, HG_DK), _BF16)
    scores = []
    for i0 in range(0, N_SUB, 2):
        lo0, lo1 = i0 * SUB_CHUNK, (i0 + 1) * SUB_CHUNK
        hi = lo1 + SUB_CHUNK
        ref0 = None if i0 == 0 else cum[lo0 - 1:lo0]
        ref1 = cum[lo1 - 1:lo1]
        q0 = qd[lo0:lo1] if i0 == 0 else queries(ref0, lo0)
        q1 = queries(ref1, lo1)
        lhs = jnp.concatenate([jnp.concatenate([q0, zeros], axis=1),
                               jnp.concatenate([zeros, q1], axis=1)], axis=0)
        rhs = jnp.concatenate([jnp.concatenate([keys(ref0, lo1), zeros], axis=0),
                               keys(ref1, hi)], axis=1)
        scores.append((lhs, rhs))
    return qd, kd, v_b, dec, scores


def _chunk_head_score_dots(operands):
    qd, kd, v_b, dec, pairs = operands
    upd = _dot_tn(kd, v_b)
    return qd, v_b, dec, upd, [_dot_nt(lhs, rhs) for lhs, rhs in pairs]


def _chunk_head_outputs(st, qd, v_b, scores):
    o_inter = _dot(qd, st.astype(_BF16))
    outs = []
    for j, sc in enumerate(scores):
        lo = 2 * j * SUB_CHUNK
        row = lax.broadcasted_iota(jnp.int32, sc.shape, 0) + lo
        col = lax.broadcasted_iota(jnp.int32, sc.shape, 1)
        sc = jnp.where(col <= row, sc, 0.0).astype(_BF16)
        outs.append(_dot(sc, v_b[0:sc.shape[1]]))
    return o_inter + jnp.concatenate(outs, axis=0)


def _block_kernel(n_seq_tiles,
                  xa_ref, xb_ref, moda_ref, modb_ref, modf_ref, lb_ref, w_in_ref, tri_ref, bgate_ref,
                  normw_ref, wpa_ref, lnw_ref, lnb_ref, ws_ref, bs_ref, wpb_ref, wout_ref, ln1w_ref,
                  ln1b_ref, w1_ref, w2_ref, ln2w_ref, ln2b_ref,
                  out_ref, st_ref, hg_ref, gm_ref, h1_ref, u2_ref, act_ref):
    s = pl.program_id(0)

    @pl.when(s == 0)
    def _():
        st_ref[...] = jnp.zeros_like(st_ref)
        h1_ref[...] = jnp.zeros_like(h1_ref)
        u2_ref[...] = jnp.zeros_like(u2_ref)

    half = TM // 2
    tri = tri_ref[...]
    norm_w = normw_ref[...]

    def chunk_head_slices(c, hh):
        return (slice(c * CHUNK, (c + 1) * CHUNK), slice(hh * HG_DK, (hh + 1) * HG_DK))

    def tile_step(tile, x_ref, mod_ref, modf_ref, slot, out_idx):
        prev = 1 - slot
        keep = jnp.where(tile % n_seq_tiles == 0, 0.0, 1.0).astype(_F32)

        def ffn_slice(i):
            u2 = u2_ref[prev]
            a = _dot(u2, _weight(w1_ref[:, i * MXU_N:(i + 1) * MXU_N]))
            b = _dot(u2, _weight(w1_ref[:, D_FF + i * MXU_N:D_FF + (i + 1) * MXU_N]))
            act_ref[:, i * MXU_N:(i + 1) * MXU_N] = (_silu(a) * b).astype(_BF16)

        def ffn_down(rows):
            return _dot(act_ref[rows, :], _weight(w2_ref[...]))

        def ffn_finish(rows, ffn):
            g2 = modf_ref[0, 5:6]
            r = DEEPNORM_ALPHA * h1_ref[prev, rows, :] + g2 * ffn
            out_ref[out_idx, rows, :] = _layer_norm(r) * ln2w_ref[...] + ln2b_ref[...]

        sh1, sc1, g1 = mod_ref[0, 0:1], mod_ref[0, 1:2], mod_ref[0, 2:3]
        for i in range(FFN_LEAD):
            ffn_slice(i)
        u = (_layer_norm(x_ref[0]) * (1.0 + sc1) + sh1).astype(_BF16)

        def stage2(z, gates):
            q, k, cum = gates
            return [[_chunk_head_operands(q[sl], k[sl], cum[sl], z[2][sl])
                     for sl in (chunk_head_slices(c, hh) for hh in range(2))]
                    for c in range(N_CHUNKS)]

        def stage3(p, s2):
            outs = []
            for hh in range(2):
                st = st_ref[2 * p + hh] * keep
                col = []
                for c in range(N_CHUNKS):
                    qd, v_b, dec, upd, scores = s2[c][hh]
                    col.append(_chunk_head_outputs(st, qd, v_b, scores))
                    st = st * dec + upd
                st_ref[2 * p + hh] = st
                outs.append(col)
            return outs

        def stage4(p, z, s3):
            for hh in range(2):
                hd = 2 * p + hh
                for c in range(N_CHUNKS):
                    sl = chunk_head_slices(c, hh)
                    o = s3[hh][c]
                    o = o * lax.rsqrt(jnp.mean(o * o, axis=-1, keepdims=True) + RMS_EPS) * norm_w
                    hg_ref[sl[0], hd * HG_DK:(hd + 1) * HG_DK] = (
                        o * _silu(z[3][sl])).astype(_BF16)

        field = {}
        z, s1, s2, operands, gates = {}, {}, {}, {}, {}
        fillers = (COL_V, COL_U, COL_GA, COL_GB)
        next_slice = FFN_LEAD
        for r in range(N_PAIRS + len(fillers)):
            p = r - 2
            if 0 <= p < N_PAIRS:
                operands[p] = stage2(z[p], s1.pop(p))
            p = r - 1
            if 0 <= p < N_PAIRS:
                lb = lb_ref[:, p * HEAD_PAIR:(p + 1) * HEAD_PAIR]
                gates[p] = _pair_gates(z[p][0], z[p][1], lb)
            if r < N_PAIRS:
                z[r] = _pair_project(u, w_in_ref, r)
            else:
                col = fillers[r - N_PAIRS]
                field[col] = _dot(u, _weight(w_in_ref[:, col:col + D_MODEL]))
            for _ in range(FFN_PER_ROW):
                if next_slice < FF_SLICES:
                    ffn_slice(next_slice)
                    next_slice += 1
            p = r - 3
            if 0 <= p < N_PAIRS:
                stage4(p, z.pop(p), stage3(p, s2.pop(p)))
            p = r - 2
            if 0 <= p < N_PAIRS:
                s2[p] = [[_chunk_head_score_dots(o) for o in row] for row in operands.pop(p)]
            p = r - 1
            if 0 <= p < N_PAIRS:
                s1[p] = _pair_cumsum(gates.pop(p), tri)
        while next_slice < FF_SLICES:
            ffn_slice(next_slice)
            next_slice += 1

        gv = _by_row_halves(
            lambda zv: (_layer_norm(_gelu(zv)) * lnw_ref[...] + lnb_ref[...]).astype(_BF16),
            field[COL_V])
        gu = _by_row_halves(_gelu, field[COL_U])
        n_blocks = TM // GM_BLOCK
        for g in range(GM_GROUPS):
            cols = slice(g * GM_CG, (g + 1) * GM_CG)
            rhs = jnp.concatenate([gv[nb * GM_BLOCK:(nb + 1) * GM_BLOCK, cols]
                                   for nb in range(n_blocks)], axis=1)
            sv = _dot(ws_ref[g], rhs)
            for nb in range(n_blocks):
                rows = slice(nb * GM_BLOCK, (nb + 1) * GM_BLOCK)
                sv_nb = sv[:, nb * GM_CG:(nb + 1) * GM_CG] + bs_ref[g]
                gm_ref[rows, cols] = (gu[rows, cols] * sv_nb).astype(_BF16)

        top, bot = slice(0, half), slice(half, TM)
        y_b = _dot(gm_ref[...], _weight(wpb_ref[...]))
        y_a = _dot(hg_ref[...], _weight(wpa_ref[...]))
        mix = _by_row_halves(
            lambda za, zb, ya, yb: (_sigmoid(za + bgate_ref[0:1]) * ya
                                    + _sigmoid(zb + bgate_ref[1:2]) * yb).astype(_BF16),
            field[COL_GA], field[COL_GB], y_a, y_b)
        m = _dot(mix, _weight(wout_ref[...]))
        ffn_top = ffn_down(top)
        ffn_bot = ffn_down(bot)
        sh2, sc2 = mod_ref[0, 3:4], mod_ref[0, 4:5]
        for rows in (top, bot):
            h1 = (_layer_norm(DEEPNORM_ALPHA * x_ref[0, rows, :] + g1 * m[rows])
                  * ln1w_ref[...] + ln1b_ref[...])
            h1_ref[slot, rows, :] = h1
            u2_ref[slot, rows, :] = (_layer_norm(h1) * (1.0 + sc2) + sh2).astype(_BF16)
        ffn_finish(top, ffn_top)
        ffn_finish(bot, ffn_bot)

    tile_step(2 * s - 1, xa_ref, moda_ref, modf_ref, 0, 0)
    tile_step(2 * s, xb_ref, modb_ref, moda_ref, 1, 1)


def _block(x, mod, consts):
    n_tiles, _, d = x.shape
    n_batch = mod.shape[0]
    n_seq_tiles = n_tiles // n_batch
    last = n_tiles - 1
    assert n_tiles % 2 == 0 and n_seq_tiles % 2 == 0

    def const_spec(a):
        return pl.BlockSpec(a.shape, lambda s: (0,) * a.ndim, pipeline_mode=pl.Buffered(1))

    def tile_spec(offset):
        return pl.BlockSpec((1, TM, d), lambda s: (jnp.clip(2 * s + offset, 0, last), 0, 0))

    def mod_spec(offset):
        return pl.BlockSpec(
            (1, 6, d), lambda s: (jnp.clip(2 * s + offset, 0, last) // n_seq_tiles, 0, 0))

    return pl.pallas_call(
        functools.partial(_block_kernel, n_seq_tiles),
        grid=(n_tiles // 2 + 1,),
        in_specs=[tile_spec(-1), tile_spec(0), mod_spec(-1), mod_spec(0), mod_spec(-2)]
        + [const_spec(a) for a in consts],
        out_specs=pl.BlockSpec((2, TM, d), lambda s: (jnp.maximum(s - 1, 0), 0, 0)),
        out_shape=jax.ShapeDtypeStruct((n_tiles, TM, d), _F32),
        scratch_shapes=[
            pltpu.VMEM((HG_HEADS, HG_DK, HG_DK), _F32),
            pltpu.VMEM((TM, d), _BF16),
            pltpu.VMEM((TM, d), _BF16),
            pltpu.VMEM((2, TM, d), _F32),
            pltpu.VMEM((2, TM, d), _BF16),
            pltpu.VMEM((TM, D_FF), _BF16),
        ],
        compiler_params=pltpu.CompilerParams(
            dimension_semantics=("arbitrary",),
            vmem_limit_bytes=VMEM_LIMIT),
        name="block",
    )(x, x, mod, mod, mod, *consts)


def _chunk_tri(n):
    r = jnp.arange(n)[:, None]
    c = jnp.arange(n)[None, :]
    return ((r >= c) & (r // CHUNK == c // CHUNK)).astype(_BF16)


def kernel(x, c, w_ada, b_ada, w_in, b_gate, hgrn_lb_logits, hgrn_norm_w, w_proj_a, gmlp_ln_w,
           gmlp_ln_b, gmlp_ws, gmlp_bs, w_proj_b, w_out, ln1_w, ln1_b, w_ffn_in, w_ffn_out,
           ln2_w, ln2_b):
    assert w_ada.shape[0] == 1, "single-layer block"
    b, s, d = x.shape
    row = lambda a: a.reshape(1, -1)
    mod, lb = _prep(c, w_ada[0], b_ada[0], hgrn_lb_logits)
    mod = mod.transpose(1, 0, 2)

    pos = jnp.arange(GM_BLOCK) // CHUNK
    ws = jnp.where(pos[:, None] >= pos[None, :], gmlp_ws[0], 0.0).astype(_BF16)
    bs = jnp.broadcast_to(gmlp_bs[0][:, :, None], (GM_GROUPS, GM_BLOCK, GM_CG))

    p_in, p_a, p_b, p_out, p_ffn_in, p_ffn_out = _pack_weights(
        w_in[0], w_proj_a[0], w_proj_b[0], w_out[0], w_ffn_in[0], w_ffn_out[0])
    consts = (lb, p_in, _chunk_tri(TM), b_gate[0], row(hgrn_norm_w[0]),
              p_a, row(gmlp_ln_w[0]), row(gmlp_ln_b[0]), ws, bs,
              p_b, p_out, row(ln1_w[0]), row(ln1_b[0]),
              p_ffn_in, p_ffn_out, row(ln2_w[0]), row(ln2_b[0]))
    out = _block(x.reshape(b * s // TM, TM, d), mod, consts)
    return out.reshape(b, s, d)
```
